```python
import jax, jax.numpy as jnp
from jax import lax

D_MODEL = 1024
BATCH = 16
SEQ = 2048
DEPTH = 2

D_MIX = D_MODEL
CHUNK = 128
A_GROUPS = 4
A_WIDTH = D_MIX // 4
A_HEAD = A_WIDTH // A_GROUPS
B_GROUPS = 4
B_WIDTH = D_MIX // 4
CONV_W = 3
C_HEADS = 4
C_WIDTH = D_MIX // 2
C_HEAD = C_WIDTH // C_HEADS
ROPE_BASE = 10000.0
D_PROJ = 2 * A_WIDTH + 3 * B_WIDTH + 4 * C_WIDTH
D_FF = 2816
MACARON = 0.5
EPS = 1e-6

kernel_name = "hybrid_gmlp_shortconv_retention_macaron"


def _rmsnorm(x, g):
    xf = x.astype(jnp.float32)
    y = xf * lax.rsqrt(jnp.mean(xf * xf, axis=-1, keepdims=True) + EPS)
    return (y * g.astype(jnp.float32)).astype(x.dtype)


def _group_layernorm(x, g, groups):
    shp = x.shape
    xf = x.astype(jnp.float32).reshape(shp[:-1] + (groups, shp[-1] // groups))
    mu = jnp.mean(xf, axis=-1, keepdims=True)
    var = jnp.mean(jnp.square(xf - mu), axis=-1, keepdims=True)
    y = ((xf - mu) * lax.rsqrt(var + EPS)).reshape(shp)
    return (y * g.astype(jnp.float32)).astype(x.dtype)


def _swiglu(x, w_gate, w_up, w_down):
    return (jax.nn.silu(x @ w_gate) * (x @ w_up)) @ w_down


def _gmlp_mixer(p, v_norm, w_s, b_s):
    bsz, s, _ = p.shape
    nc = s // CHUNK
    z = jax.nn.gelu(p)
    u, v = z[..., :A_WIDTH], z[..., A_WIDTH:]
    v = _group_layernorm(v, v_norm, A_GROUPS)
    v = v.reshape(bsz, nc, CHUNK, A_GROUPS, A_HEAD)
    causal = jnp.tril(jnp.ones((CHUNK, CHUNK), dtype=w_s.dtype))
    ws = w_s * causal[None]
    sv = jnp.einsum('gts,bnsgd->bntgd', ws, v) + b_s.T[None, None, :, :, None]
    return u * sv.reshape(bsz, s, A_WIDTH)


def _shortconv_mixer(p, conv_w):
    s = p.shape[1]
    bg, cg, xin = p[..., :B_WIDTH], p[..., B_WIDTH:2 * B_WIDTH], p[..., 2 * B_WIDTH:]
    z = cg * xin
    zp = jnp.pad(z, ((0, 0), (CONV_W - 1, 0), (0, 0)))
    y = sum(zp[:, i:i + s, :] * conv_w[i] for i in range(CONV_W))
    return bg * y


def _rotary(x, pos):
    half = x.shape[-1] // 2
    inv = ROPE_BASE ** (-jnp.arange(half, dtype=jnp.float32) / half)
    ang = pos[:, None] * inv[None, :]
    cos = jnp.cos(ang)[None, :, None, :]
    sin = jnp.sin(ang)[None, :, None, :]
    x1, x2 = x[..., :half], x[..., half:]
    return jnp.concatenate([x1 * cos - x2 * sin, x2 * cos + x1 * sin], axis=-1)


def _retention_mixer(p, gn):
    dtype = p.dtype
    bsz, s, _ = p.shape
    nc = s // CHUNK
    pf = p.astype(jnp.float32)
    q = pf[..., 0 * C_WIDTH:1 * C_WIDTH].reshape(bsz, s, C_HEADS, C_HEAD)
    k = pf[..., 1 * C_WIDTH:2 * C_WIDTH].reshape(bsz, s, C_HEADS, C_HEAD)
    v = pf[..., 2 * C_WIDTH:3 * C_WIDTH].reshape(bsz, s, C_HEADS, C_HEAD)
    g = p[..., 3 * C_WIDTH:]
    pos = jnp.arange(s, dtype=jnp.float32)
    q = _rotary(q, pos)
    k = _rotary(k, pos) * (C_HEAD ** -0.5)

    log_g = jnp.log1p(-jnp.exp2(-5.0 - jnp.arange(C_HEADS, dtype=jnp.float32)))
    idx = jnp.arange(CHUNK, dtype=jnp.float32)
    diff = idx[:, None] - idx[None, :]
    decay = jnp.where(diff[None] >= 0,
                      jnp.exp(log_g[:, None, None] * jnp.maximum(diff, 0.0)[None]), 0.0)
    zeta = jnp.exp(log_g[None, :] * (CHUNK - 1 - idx)[:, None])
    xi = jnp.exp(log_g[None, :] * (idx + 1.0)[:, None])
    gamma_chunk = jnp.exp(log_g * CHUNK)

    qc = q.reshape(bsz, nc, CHUNK, C_HEADS, C_HEAD)
    kc = k.reshape(bsz, nc, CHUNK, C_HEADS, C_HEAD)
    vc = v.reshape(bsz, nc, CHUNK, C_HEADS, C_HEAD)

    scores = jnp.einsum('bnthd,bnshd->bnhts', qc, kc) * decay[None, None]
    inner = jnp.einsum('bnhts,bnshv->bnthv', scores, vc)

    chunk_kv = jnp.einsum('bnshd,sh,bnshv->bnhdv', kc, zeta, vc)

    def step(state, kv):
        return state * gamma_chunk[None, :, None, None] + kv, state

    init = jnp.zeros((bsz, C_HEADS, C_HEAD, C_HEAD), jnp.float32)
    _, prev = lax.scan(step, init, jnp.moveaxis(chunk_kv, 1, 0))
    cross = jnp.einsum('bnthd,nbhdv,th->bnthv', qc, prev, xi)

    ret = (inner + cross).reshape(bsz, s, C_WIDTH)
    ret = _group_layernorm(ret, gn, C_HEADS).astype(dtype)
    return jax.nn.silu(g) * ret


def _layer(x, ffn1_norm, ffn1_w_gate, ffn1_w_up, ffn1_w_down, mix_norm, w_in,
           gmlp_v_norm, gmlp_w_s, gmlp_b_s, conv_w, ret_gn, w_out,
           ffn2_norm, ffn2_w_gate, ffn2_w_up, ffn2_w_down):
    x = x + MACARON * _swiglu(_rmsnorm(x, ffn1_norm), ffn1_w_gate, ffn1_w_up, ffn1_w_down)
    h = _rmsnorm(x, mix_norm)
    p = h @ w_in
    a_end = 2 * A_WIDTH
    b_end = a_end + 3 * B_WIDTH
    y_a = _gmlp_mixer(p[..., :a_end], gmlp_v_norm, gmlp_w_s, gmlp_b_s)
    y_b = _shortconv_mixer(p[..., a_end:b_end], conv_w)
    y_c = _retention_mixer(p[..., b_end:], ret_gn)
    x = x + jnp.concatenate([y_a, y_b, y_c], axis=-1) @ w_out
    x = x + MACARON * _swiglu(_rmsnorm(x, ffn2_norm), ffn2_w_gate, ffn2_w_up, ffn2_w_down)
    return x


def setup_inputs(seed: int = 0) -> dict:
    key = jax.random.key(seed)
    ks = jax.random.split(key, 18)
    f32 = jnp.float32

    def w(k, shape, fan_in):
        return jax.random.normal(k, shape, f32) * (fan_in ** -0.5)

    def gain(k, shape):
        return 1.0 + 0.1 * jax.random.normal(k, shape, f32)

    return {
        "x": jax.random.normal(ks[0], (BATCH, SEQ, D_MODEL), f32),
        "ffn1_norm": gain(ks[1], (DEPTH, D_MODEL)),
        "ffn1_w_gate": w(ks[2], (DEPTH, D_MODEL, D_FF), D_MODEL),
        "ffn1_w_up": w(ks[3], (DEPTH, D_MODEL, D_FF), D_MODEL),
        "ffn1_w_down": w(ks[4], (DEPTH, D_FF, D_MODEL), D_FF),
        "mix_norm": gain(ks[5], (DEPTH, D_MODEL)),
        "w_in": w(ks[6], (DEPTH, D_MODEL, D_PROJ), D_MODEL),
        "gmlp_v_norm": gain(ks[7], (DEPTH, A_WIDTH)),
        "gmlp_w_s": w(ks[8], (DEPTH, A_GROUPS, CHUNK, CHUNK), CHUNK),
        "gmlp_b_s": 1.0 + 0.1 * jax.random.normal(ks[9], (DEPTH, A_GROUPS, CHUNK), f32),
        "conv_w": w(ks[10], (DEPTH, CONV_W, B_WIDTH), CONV_W),
        "ret_gn": gain(ks[11], (DEPTH, C_WIDTH)),
        "w_out": w(ks[12], (DEPTH, D_MIX, D_MODEL), D_MIX),
        "ffn2_norm": gain(ks[13], (DEPTH, D_MODEL)),
        "ffn2_w_gate": w(ks[14], (DEPTH, D_MODEL, D_FF), D_MODEL),
        "ffn2_w_up": w(ks[15], (DEPTH, D_MODEL, D_FF), D_MODEL),
        "ffn2_w_down": w(ks[16], (DEPTH, D_FF, D_MODEL), D_FF),
        "final_norm": gain(ks[17], (D_MODEL,)),
    }


def reference(x, ffn1_norm, ffn1_w_gate, ffn1_w_up, ffn1_w_down, mix_norm, w_in,
              gmlp_v_norm, gmlp_w_s, gmlp_b_s, conv_w, ret_gn, w_out,
              ffn2_norm, ffn2_w_gate, ffn2_w_up, ffn2_w_down, final_norm):
    for l in range(DEPTH):
        x = _layer(x, ffn1_norm[l], ffn1_w_gate[l], ffn1_w_up[l], ffn1_w_down[l],
                   mix_norm[l], w_in[l], gmlp_v_norm[l], gmlp_w_s[l], gmlp_b_s[l],
                   conv_w[l], ret_gn[l], w_out[l],
                   ffn2_norm[l], ffn2_w_gate[l], ffn2_w_up[l], ffn2_w_down[l])
    return _rmsnorm(x, final_norm)
```

```python
import functools

import jax
import jax.numpy as jnp
from jax import lax
from jax.experimental import pallas as pl
from jax.experimental.pallas import tpu as pltpu

D_MODEL = 1024
CHUNK = 128
A_GROUPS = 4
A_WIDTH = 256
A_HEAD = A_WIDTH // A_GROUPS
B_WIDTH = 256
CONV_W = 3
C_HEADS = 4
C_WIDTH = 512
C_HEAD = C_WIDTH // C_HEADS
ROPE_BASE = 10000.0
D_FF = 2816
MACARON = 0.5
EPS = 1e-6

A_END = 2 * A_WIDTH
B_END = A_END + 3 * B_WIDTH
D_PROJ = B_END + 4 * C_WIDTH

V7X_VMEM_BYTES = 64 * 1024 * 1024
V7X_MXU_DIM = 256
V7X_SUBLANES = 8

FFN_TOKENS = 512
FFN_CHUNK = V7X_MXU_DIM
MIX_TOKENS = 512
CONV_HALO = V7X_SUBLANES


def _vmem_limit(block_bytes, scratch_bytes, temp_bytes):
    want = 2 * block_bytes + scratch_bytes + temp_bytes
    return int(min(want, V7X_VMEM_BYTES - 8 * 1024 * 1024))


def _rms(x, gain):
    return x * lax.rsqrt(jnp.mean(x * x, axis=-1, keepdims=True) + EPS) * gain


def _ffn_kernel(x_ref, norm_ref, wgu_ref, wd_ref, fnorm_ref, o_ref, *, apply_final_norm):
    x = x_ref[...]
    h = _rms(x, norm_ref[...]).astype(jnp.bfloat16)
    acc = jnp.zeros(x.shape, jnp.float32)
    for c in range(D_FF // FFN_CHUNK):
        gu = jnp.dot(h, wgu_ref[:, 2 * c * FFN_CHUNK:2 * (c + 1) * FFN_CHUNK],
                     preferred_element_type=jnp.float32)
        a = (jax.nn.silu(gu[:, :FFN_CHUNK]) * gu[:, FFN_CHUNK:]).astype(jnp.bfloat16)
        acc = acc + jnp.dot(a, wd_ref[c * FFN_CHUNK:(c + 1) * FFN_CHUNK, :],
                            preferred_element_type=jnp.float32)
    y = x + MACARON * acc
    if apply_final_norm:
        y = _rms(y, fnorm_ref[...])
    o_ref[...] = y


def _ffn(x2, norm, wgu, wd, fnorm, *, apply_final_norm):
    n_tok = x2.shape[0]
    row = lambda i: (i, 0)
    fixed = lambda i: (0, 0)
    blocks = FFN_TOKENS * D_MODEL * 4 * 2 + (wgu.size + wd.size) * 2
    temps = FFN_TOKENS * (D_MODEL * 4 * 3 + 2 * FFN_CHUNK * 4 * 4)
    return pl.pallas_call(
        functools.partial(_ffn_kernel, apply_final_norm=apply_final_norm),
        grid=(n_tok // FFN_TOKENS,),
        in_specs=[
            pl.BlockSpec((FFN_TOKENS, D_MODEL), row),
            pl.BlockSpec((1, D_MODEL), fixed),
            pl.BlockSpec(wgu.shape, fixed),
            pl.BlockSpec(wd.shape, fixed),
            pl.BlockSpec((1, D_MODEL), fixed),
        ],
        out_specs=pl.BlockSpec((FFN_TOKENS, D_MODEL), row),
        out_shape=jax.ShapeDtypeStruct(x2.shape, jnp.float32),
        compiler_params=pltpu.CompilerParams(
            dimension_semantics=("parallel",),
            vmem_limit_bytes=_vmem_limit(blocks, 0, temps)),
        name="ffn_half_step",
    )(x2, norm, wgu, wd, fnorm)


def _split_bf16(v):
    hi = v.astype(jnp.bfloat16)
    lo = (v - hi.astype(jnp.float32)).astype(jnp.bfloat16)
    return hi, lo


def _group_mean(v, avg):
    hi, lo = _split_bf16(v)
    return (jnp.dot(hi, avg, preferred_element_type=jnp.float32)
            + jnp.dot(lo, avg, preferred_element_type=jnp.float32))


def _lane_norm(x, gain):
    mu = jnp.mean(x, axis=-1, keepdims=True)
    d = x - mu
    var = jnp.mean(d * d, axis=-1, keepdims=True)
    return d * lax.rsqrt(var + EPS) * gain


def _mixer_kernel(x_ref, norm_ref, win_ref, wout_ref, vnorm_ref, ws_ref, bias_ref, convw_ref, gn_ref,
                  rope_ref, ret_ref, o_ref, state_ref, zbuf_ref, y_ref):
    f32, bf16 = jnp.float32, jnp.bfloat16
    T = MIX_TOKENS
    first = pl.program_id(1) == 0

    @pl.when(first)
    def _():
        state_ref[...] = jnp.zeros(state_ref.shape, f32)
        zbuf_ref[0:CONV_HALO, :] = jnp.zeros((CONV_HALO, B_WIDTH), f32)

    x = x_ref[...]
    h = _rms(x, norm_ref[...]).astype(bf16)

    z = jax.nn.gelu(jnp.dot(h, win_ref[:, 0:A_END], preferred_element_type=f32), approximate=True)
    u, v = z[:, :A_WIDTH], z[:, A_WIDTH:]
    lane_grp = lax.shift_right_logical(lax.broadcasted_iota(jnp.int32, (A_WIDTH, A_WIDTH), 1), 6)
    row_grp = lax.shift_right_logical(lax.broadcasted_iota(jnp.int32, (A_WIDTH, A_WIDTH), 0), 6)
    avg = jnp.where(lane_grp == row_grp, 1.0 / A_HEAD, 0.0).astype(bf16)
    mu = _group_mean(v, avg)
    d = v - mu
    var = _group_mean(d * d, avg)
    vn = (d * lax.rsqrt(var + EPS) * vnorm_ref[...]).astype(bf16)
    t_idx = lax.broadcasted_iota(jnp.int32, (CHUNK, A_GROUPS * CHUNK), 0)
    s_idx = lax.broadcasted_iota(jnp.int32, (CHUNK, A_GROUPS * CHUNK), 1) & (CHUNK - 1)
    ws = jnp.where(s_idx <= t_idx, ws_ref[...], 0.0).astype(bf16)
    out_grp = lax.shift_right_logical(lax.broadcasted_iota(jnp.int32, (CHUNK, A_WIDTH), 1), 6)
    for n in range(T // CHUNK):
        rows = slice(n * CHUNK, (n + 1) * CHUNK)
        vc = vn[rows]
        rhs = jnp.concatenate([jnp.where(out_grp == g, vc, jnp.zeros_like(vc)) for g in range(A_GROUPS)],
                              axis=0)
        sv = jnp.dot(ws, rhs, preferred_element_type=f32) + bias_ref[...]
        y_ref[rows, 0:A_WIDTH] = (u[rows] * sv).astype(bf16)

    pb = jnp.dot(h, win_ref[:, A_END:B_END], preferred_element_type=f32)
    bg, cg, xin = pb[:, :B_WIDTH], pb[:, B_WIDTH:2 * B_WIDTH], pb[:, 2 * B_WIDTH:]
    zc = cg * xin
    zbuf_ref[CONV_HALO:CONV_HALO + T, :] = zc
    conv = zc * convw_ref[CONV_W - 1:CONV_W, :]
    for i in range(CONV_W - 1):
        lag = CONV_W - 1 - i
        conv = conv + zbuf_ref[CONV_HALO - lag:CONV_HALO - lag + T, :] * convw_ref[i:i + 1, :]
    y_ref[:, A_WIDTH:A_WIDTH + B_WIDTH] = (bg * conv).astype(bf16)
    zbuf_ref[0:CONV_HALO, :] = zbuf_ref[T:T + CONV_HALO, :]

    pc = jnp.dot(h, win_ref[:, B_END:D_PROJ], preferred_element_type=f32)
    cos_q, sin_q, cos_k, sin_k = rope_ref[0], rope_ref[1], rope_ref[2], rope_ref[3]
    for hd in range(C_HEADS):
        lanes = slice(hd * C_HEAD, (hd + 1) * C_HEAD)
        decay, zeta, xi, gamma = ret_ref[0, hd], ret_ref[1, hd], ret_ref[2, hd], ret_ref[3, hd]
        qh = pc[:, lanes]
        kh = pc[:, C_WIDTH + hd * C_HEAD:C_WIDTH + (hd + 1) * C_HEAD]
        vh = pc[:, 2 * C_WIDTH + hd * C_HEAD:2 * C_WIDTH + (hd + 1) * C_HEAD]
        gh = pc[:, 3 * C_WIDTH + hd * C_HEAD:3 * C_WIDTH + (hd + 1) * C_HEAD]
        qh = qh * cos_q + pltpu.roll(qh, C_HEAD // 2, 1) * sin_q
        kh = kh * cos_k + pltpu.roll(kh, C_HEAD // 2, 1) * sin_k
        gate = jax.nn.silu(gh)
        gn = gn_ref[:, lanes]
        state = state_ref[hd]
        for n in range(T // CHUNK):
            rows = slice(n * CHUNK, (n + 1) * CHUNK)
            qc, kc, vc = qh[rows], kh[rows], vh[rows].astype(bf16)
            qb = qc.astype(bf16)
            scores = lax.dot_general(qb, kc.astype(bf16), (((1,), (1,)), ((), ())),
                                     preferred_element_type=f32) * decay
            inner = jnp.dot(scores.astype(bf16), vc, preferred_element_type=f32)
            cross = jnp.dot(qb, state.astype(bf16), preferred_element_type=f32) * xi
            kv = lax.dot_general((kc * zeta).astype(bf16), vc, (((0,), (0,)), ((), ())),
                                 preferred_element_type=f32)
            state = state * gamma + kv
            ret = _lane_norm(inner + cross, gn)
            y_ref[rows, A_WIDTH + B_WIDTH + hd * C_HEAD:A_WIDTH + B_WIDTH + (hd + 1) * C_HEAD] = (
                gate[rows] * ret).astype(bf16)
        state_ref[hd] = state

    o_ref[...] = x + jnp.dot(y_ref[...], wout_ref[...], preferred_element_type=f32)


def _mixer(x3, norm, win, wout, vnorm, ws, bias, convw, gn, rope, ret):
    bsz, seq, _ = x3.shape
    tile = lambda b, j: (b, j, 0)
    fixed2 = lambda b, j: (0, 0)
    blocks = (MIX_TOKENS * D_MODEL * 4 * 2 + (win.size + wout.size) * 2 + ws.size * 4 + bias.size * 4
              + 4 * MIX_TOKENS * C_HEAD * 4 + ret.size * 4)
    scratch = C_HEADS * C_HEAD * C_HEAD * 4 + (CONV_HALO + MIX_TOKENS) * B_WIDTH * 4 + MIX_TOKENS * D_MODEL * 2
    temps = MIX_TOKENS * (D_PROJ * 4 + D_MODEL * 4 * 2) + 4 * 1024 * 1024
    return pl.pallas_call(
        _mixer_kernel,
        grid=(bsz, seq // MIX_TOKENS),
        in_specs=[
            pl.BlockSpec((None, MIX_TOKENS, D_MODEL), tile),
            pl.BlockSpec((1, D_MODEL), fixed2),
            pl.BlockSpec(win.shape, fixed2),
            pl.BlockSpec(wout.shape, fixed2),
            pl.BlockSpec((1, A_WIDTH), fixed2),
            pl.BlockSpec(ws.shape, fixed2),
            pl.BlockSpec(bias.shape, fixed2),
            pl.BlockSpec(convw.shape, fixed2),
            pl.BlockSpec((1, C_WIDTH), fixed2),
            pl.BlockSpec((4, MIX_TOKENS, C_HEAD), lambda b, j: (0, j, 0)),
            pl.BlockSpec(ret.shape, lambda b, j: (0, 0, 0, 0)),
        ],
        out_specs=pl.BlockSpec((None, MIX_TOKENS, D_MODEL), tile),
        out_shape=jax.ShapeDtypeStruct(x3.shape, jnp.float32),
        scratch_shapes=[
            pltpu.VMEM((C_HEADS, C_HEAD, C_HEAD), jnp.float32),
            pltpu.VMEM((CONV_HALO + MIX_TOKENS, B_WIDTH), jnp.float32),
            pltpu.VMEM((MIX_TOKENS, D_MODEL), jnp.bfloat16),
        ],
        compiler_params=pltpu.CompilerParams(
            dimension_semantics=("parallel", "arbitrary"),
            vmem_limit_bytes=_vmem_limit(blocks, scratch, temps)),
        name="mixer_block",
    )(x3, norm, win, wout, vnorm, ws, bias, convw, gn, rope, ret)


def _rope_tables(seq):
    half = C_HEAD // 2
    inv = ROPE_BASE ** (-jnp.arange(half, dtype=jnp.float32) / half)
    ang = jnp.arange(seq, dtype=jnp.float32)[:, None] * inv[None, :]
    cos, sin = jnp.cos(ang), jnp.sin(ang)
    cc = jnp.concatenate([cos, cos], axis=-1)
    ss = jnp.concatenate([-sin, sin], axis=-1)
    scale = C_HEAD ** -0.5
    return jnp.stack([cc, ss, cc * scale, ss * scale])


def _retention_tables():
    log_g = jnp.log1p(-jnp.exp2(-5.0 - jnp.arange(C_HEADS, dtype=jnp.float32)))
    idx = jnp.arange(CHUNK, dtype=jnp.float32)
    diff = idx[:, None] - idx[None, :]
    decay = jnp.where(diff[None] >= 0,
                      jnp.exp(log_g[:, None, None] * jnp.maximum(diff, 0.0)[None]), 0.0)
    zeta = jnp.exp(log_g[:, None] * (CHUNK - 1 - idx)[None, :])
    xi = jnp.exp(log_g[:, None] * (idx + 1.0)[None, :])
    gamma = jnp.exp(log_g * CHUNK)
    full = (C_HEADS, CHUNK, CHUNK)
    return jnp.stack([decay,
                      jnp.broadcast_to(zeta[:, :, None], full),
                      jnp.broadcast_to(xi[:, :, None], full),
                      jnp.broadcast_to(gamma[:, None, None], full)])


def _interleave_gate_up(w_gate, w_up):
    d, f = w_gate.shape
    n = f // FFN_CHUNK
    both = jnp.stack([w_gate.reshape(d, n, FFN_CHUNK), w_up.reshape(d, n, FFN_CHUNK)], axis=2)
    return both.reshape(d, 2 * f).astype(jnp.bfloat16)


def kernel(x, ffn1_norm, ffn1_w_gate, ffn1_w_up, ffn1_w_down, mix_norm, w_in, gmlp_v_norm, gmlp_w_s,
           gmlp_b_s, conv_w, ret_gn, w_out, ffn2_norm, ffn2_w_gate, ffn2_w_up, ffn2_w_down, final_norm):
    bsz, seq, d = x.shape
    depth = w_in.shape[0]
    assert d == D_MODEL and seq % MIX_TOKENS == 0 and (bsz * seq) % FFN_TOKENS == 0
    bf16 = jnp.bfloat16
    rope = _rope_tables(seq)
    ret = _retention_tables()
    fnorm = final_norm.reshape(1, d)
    for l in range(depth):
        x = _ffn(x.reshape(bsz * seq, d), ffn1_norm[l].reshape(1, d),
                 _interleave_gate_up(ffn1_w_gate[l], ffn1_w_up[l]), ffn1_w_down[l].astype(bf16),
                 fnorm, apply_final_norm=False).reshape(bsz, seq, d)
        ws = jnp.transpose(gmlp_w_s[l], (1, 0, 2)).reshape(CHUNK, A_GROUPS * CHUNK)
        bias = jnp.repeat(gmlp_b_s[l].T, A_HEAD, axis=1)
        x = _mixer(x, mix_norm[l].reshape(1, d), w_in[l].astype(bf16), w_out[l].astype(bf16),
                   gmlp_v_norm[l].reshape(1, A_WIDTH), ws, bias, conv_w[l], ret_gn[l].reshape(1, C_WIDTH),
                   rope, ret)
        x = _ffn(x.reshape(bsz * seq, d), ffn2_norm[l].reshape(1, d),
                 _interleave_gate_up(ffn2_w_gate[l], ffn2_w_up[l]), ffn2_w_down[l].astype(bf16),
                 fnorm, apply_final_norm=(l == depth - 1)).reshape(bsz, seq, d)
    return x
```

```python
import functools

import jax
import jax.numpy as jnp
import numpy as np
from jax import lax
from jax.experimental import pallas as pl
from jax.experimental.pallas import tpu as pltpu

D_MODEL = 1024
CHUNK = 128
A_GROUPS = 4
A_WIDTH = 256
A_HEAD = A_WIDTH // A_GROUPS
B_WIDTH = 256
CONV_W = 3
C_HEADS = 4
C_WIDTH = 512
C_HEAD = C_WIDTH // C_HEADS
ROPE_BASE = 10000.0
D_FF = 2816
MACARON = 0.5
EPS = 1e-6

A_END = 2 * A_WIDTH
B_END = A_END + 3 * B_WIDTH
D_PROJ = B_END + 4 * C_WIDTH

V7X_VMEM_BYTES = 64 * 1024 * 1024
V7X_MXU_DIM = 256
V7X_SUBLANES = 8

FFN_TOKENS = 512
FFN_CHUNK = V7X_MXU_DIM
MIX_TOKENS = 512
CONV_HALO = V7X_SUBLANES


def _vmem_limit(block_bytes, scratch_bytes, temp_bytes):
    want = 2 * block_bytes + scratch_bytes + temp_bytes
    return int(min(want, V7X_VMEM_BYTES - 8 * 1024 * 1024))


def _rms(x, gain):
    return x * lax.rsqrt(jnp.mean(x * x, axis=-1, keepdims=True) + EPS) * gain


def _ffn_kernel(x_ref, norm_ref, wg_ref, wu_ref, wd_ref, fnorm_ref, o_ref, *, apply_final_norm):
    x = x_ref[...]
    h = _rms(x, norm_ref[...]).astype(jnp.bfloat16)
    acc = jnp.zeros(x.shape, jnp.float32)
    for c in range(D_FF // FFN_CHUNK):
        cols = slice(c * FFN_CHUNK, (c + 1) * FFN_CHUNK)
        gate = jnp.dot(h, wg_ref[:, cols], preferred_element_type=jnp.float32)
        up = jnp.dot(h, wu_ref[:, cols], preferred_element_type=jnp.float32)
        a = (jax.nn.silu(gate) * up).astype(jnp.bfloat16)
        acc = acc + jnp.dot(a, wd_ref[cols, :], preferred_element_type=jnp.float32)
    y = x + MACARON * acc
    if apply_final_norm:
        y = _rms(y, fnorm_ref[...])
    o_ref[...] = y


def _ffn(x2, layer, norm, wg, wu, wd, fnorm, *, apply_final_norm):
    n_tok = x2.shape[0]
    row = lambda i: (i, 0)
    pick = lambda i: (layer, 0, 0)
    blocks = FFN_TOKENS * D_MODEL * 4 * 2 + 3 * D_MODEL * D_FF * 2
    temps = FFN_TOKENS * (D_MODEL * 4 * 3 + 2 * FFN_CHUNK * 4 * 4)
    return pl.pallas_call(
        functools.partial(_ffn_kernel, apply_final_norm=apply_final_norm),
        grid=(n_tok // FFN_TOKENS,),
        in_specs=[
            pl.BlockSpec((FFN_TOKENS, D_MODEL), row),
            pl.BlockSpec((None, 1, D_MODEL), pick),
            pl.BlockSpec((None, D_MODEL, D_FF), pick),
            pl.BlockSpec((None, D_MODEL, D_FF), pick),
            pl.BlockSpec((None, D_FF, D_MODEL), pick),
            pl.BlockSpec((1, D_MODEL), lambda i: (0, 0)),
        ],
        out_specs=pl.BlockSpec((FFN_TOKENS, D_MODEL), row),
        out_shape=jax.ShapeDtypeStruct(x2.shape, jnp.float32),
        compiler_params=pltpu.CompilerParams(
            dimension_semantics=("parallel",),
            vmem_limit_bytes=_vmem_limit(blocks, 0, temps)),
        name="ffn_half_step",
    )(x2, norm, wg, wu, wd, fnorm)


def _split_bf16(v):
    hi = v.astype(jnp.bfloat16)
    lo = (v - hi.astype(jnp.float32)).astype(jnp.bfloat16)
    return hi, lo


def _group_mean(v, avg):
    hi, lo = _split_bf16(v)
    return (jnp.dot(hi, avg, preferred_element_type=jnp.float32)
            + jnp.dot(lo, avg, preferred_element_type=jnp.float32))


def _lane_norm(x, gain):
    mu = jnp.mean(x, axis=-1, keepdims=True)
    d = x - mu
    var = jnp.mean(d * d, axis=-1, keepdims=True)
    return d * lax.rsqrt(var + EPS) * gain


def _mixer_kernel(x_ref, norm_ref, win_ref, wout_ref, vnorm_ref, ws_ref, bias_ref, convw_ref, gn_ref,
                  rope_ref, ret_ref, o_ref, state_ref, zbuf_ref, y_ref):
    f32, bf16 = jnp.float32, jnp.bfloat16
    T = MIX_TOKENS
    first = pl.program_id(1) == 0

    @pl.when(first)
    def _():
        state_ref[...] = jnp.zeros(state_ref.shape, f32)
        zbuf_ref[0:CONV_HALO, :] = jnp.zeros((CONV_HALO, B_WIDTH), f32)

    x = x_ref[...]
    h = _rms(x, norm_ref[...]).astype(bf16)

    z = jax.nn.gelu(jnp.dot(h, win_ref[:, 0:A_END], preferred_element_type=f32), approximate=True)
    u, v = z[:, :A_WIDTH], z[:, A_WIDTH:]
    head_shift = A_HEAD.bit_length() - 1
    lane_grp = lax.shift_right_logical(lax.broadcasted_iota(jnp.int32, (A_WIDTH, A_WIDTH), 1), head_shift)
    row_grp = lax.shift_right_logical(lax.broadcasted_iota(jnp.int32, (A_WIDTH, A_WIDTH), 0), head_shift)
    avg = jnp.where(lane_grp == row_grp, 1.0 / A_HEAD, 0.0).astype(bf16)
    mu = _group_mean(v, avg)
    d = v - mu
    var = _group_mean(d * d, avg)
    vn = (d * lax.rsqrt(var + EPS) * vnorm_ref[...]).astype(bf16)
    t_idx = lax.broadcasted_iota(jnp.int32, (CHUNK, CHUNK), 0)
    s_idx = lax.broadcasted_iota(jnp.int32, (CHUNK, CHUNK), 1)
    ws = jnp.concatenate([jnp.where(s_idx <= t_idx, ws_ref[g], 0.0) for g in range(A_GROUPS)],
                         axis=1).astype(bf16)
    out_grp = lax.shift_right_logical(lax.broadcasted_iota(jnp.int32, (CHUNK, A_WIDTH), 1), head_shift)
    for n in range(T // CHUNK):
        rows = slice(n * CHUNK, (n + 1) * CHUNK)
        vc = vn[rows]
        rhs = jnp.concatenate([jnp.where(out_grp == g, vc, jnp.zeros_like(vc)) for g in range(A_GROUPS)],
                              axis=0)
        sv = jnp.dot(ws, rhs, preferred_element_type=f32) + bias_ref[...]
        y_ref[rows, 0:A_WIDTH] = (u[rows] * sv).astype(bf16)

    pb = jnp.dot(h, win_ref[:, A_END:B_END], preferred_element_type=f32)
    bg, cg, xin = pb[:, :B_WIDTH], pb[:, B_WIDTH:2 * B_WIDTH], pb[:, 2 * B_WIDTH:]
    zc = cg * xin
    zbuf_ref[CONV_HALO:CONV_HALO + T, :] = zc
    conv = zc * convw_ref[CONV_W - 1:CONV_W, :]
    for i in range(CONV_W - 1):
        lag = CONV_W - 1 - i
        conv = conv + zbuf_ref[CONV_HALO - lag:CONV_HALO - lag + T, :] * convw_ref[i:i + 1, :]
    y_ref[:, A_WIDTH:A_WIDTH + B_WIDTH] = (bg * conv).astype(bf16)
    zbuf_ref[0:CONV_HALO, :] = zbuf_ref[T:T + CONV_HALO, :]

    pc = jnp.dot(h, win_ref[:, B_END:D_PROJ], preferred_element_type=f32)
    cos_q, sin_q, cos_k, sin_k = rope_ref[0], rope_ref[1], rope_ref[2], rope_ref[3]
    for hd in range(C_HEADS):
        lanes = slice(hd * C_HEAD, (hd + 1) * C_HEAD)
        decay, zeta, xi, gamma = ret_ref[0, hd], ret_ref[1, hd], ret_ref[2, hd], ret_ref[3, hd]
        qh = pc[:, lanes]
        kh = pc[:, C_WIDTH + hd * C_HEAD:C_WIDTH + (hd + 1) * C_HEAD]
        vh = pc[:, 2 * C_WIDTH + hd * C_HEAD:2 * C_WIDTH + (hd + 1) * C_HEAD]
        gh = pc[:, 3 * C_WIDTH + hd * C_HEAD:3 * C_WIDTH + (hd + 1) * C_HEAD]
        qh = qh * cos_q + pltpu.roll(qh, C_HEAD // 2, 1) * sin_q
        kh = kh * cos_k + pltpu.roll(kh, C_HEAD // 2, 1) * sin_k
        gate = jax.nn.silu(gh)
        gn = gn_ref[:, lanes]
        state = state_ref[hd]
        for n in range(T // CHUNK):
            rows = slice(n * CHUNK, (n + 1) * CHUNK)
            qc, kc, vc = qh[rows], kh[rows], vh[rows].astype(bf16)
            qb = qc.astype(bf16)
            scores = lax.dot_general(qb, kc.astype(bf16), (((1,), (1,)), ((), ())),
                                     preferred_element_type=f32) * decay
            inner = jnp.dot(scores.astype(bf16), vc, preferred_element_type=f32)
            cross = jnp.dot(qb, state.astype(bf16), preferred_element_type=f32) * xi
            kv = lax.dot_general((kc * zeta).astype(bf16), vc, (((0,), (0,)), ((), ())),
                                 preferred_element_type=f32)
            state = state * gamma + kv
            ret = _lane_norm(inner + cross, gn)
            y_ref[rows, A_WIDTH + B_WIDTH + hd * C_HEAD:A_WIDTH + B_WIDTH + (hd + 1) * C_HEAD] = (
                gate[rows] * ret).astype(bf16)
        state_ref[hd] = state

    o_ref[...] = x + jnp.dot(y_ref[...], wout_ref[...], preferred_element_type=f32)


def _mixer(x3, layer, norm, win, wout, vnorm, ws, bias, convw, gn, rope, ret):
    bsz, seq, _ = x3.shape
    tile = lambda b, j: (b, j, 0)
    pick3 = lambda b, j: (layer, 0, 0)
    blocks = (MIX_TOKENS * D_MODEL * 4 * 2 + (D_MODEL * D_PROJ + D_MODEL * D_MODEL) * 2
              + A_GROUPS * CHUNK * CHUNK * 4 + CHUNK * A_WIDTH * 4 + 4 * MIX_TOKENS * C_HEAD * 4 + ret.size * 4)
    scratch = C_HEADS * C_HEAD * C_HEAD * 4 + (CONV_HALO + MIX_TOKENS) * B_WIDTH * 4 + MIX_TOKENS * D_MODEL * 2
    temps = MIX_TOKENS * (D_PROJ * 4 + D_MODEL * 4 * 2) + 4 * 1024 * 1024
    return pl.pallas_call(
        _mixer_kernel,
        grid=(bsz, seq // MIX_TOKENS),
        in_specs=[
            pl.BlockSpec((None, MIX_TOKENS, D_MODEL), tile),
            pl.BlockSpec((None, 1, D_MODEL), pick3),
            pl.BlockSpec((None, D_MODEL, D_PROJ), pick3),
            pl.BlockSpec((None, D_MODEL, D_MODEL), pick3),
            pl.BlockSpec((None, 1, A_WIDTH), pick3),
            pl.BlockSpec((None, A_GROUPS, CHUNK, CHUNK), lambda b, j: (layer, 0, 0, 0)),
            pl.BlockSpec((None, CHUNK, A_WIDTH), pick3),
            pl.BlockSpec((None, CONV_W, B_WIDTH), pick3),
            pl.BlockSpec((None, 1, C_WIDTH), pick3),
            pl.BlockSpec((4, MIX_TOKENS, C_HEAD), lambda b, j: (0, j, 0)),
            pl.BlockSpec(ret.shape, lambda b, j: (0, 0, 0, 0)),
        ],
        out_specs=pl.BlockSpec((None, MIX_TOKENS, D_MODEL), tile),
        out_shape=jax.ShapeDtypeStruct(x3.shape, jnp.float32),
        scratch_shapes=[
            pltpu.VMEM((C_HEADS, C_HEAD, C_HEAD), jnp.float32),
            pltpu.VMEM((CONV_HALO + MIX_TOKENS, B_WIDTH), jnp.float32),
            pltpu.VMEM((MIX_TOKENS, D_MODEL), jnp.bfloat16),
        ],
        compiler_params=pltpu.CompilerParams(
            dimension_semantics=("parallel", "arbitrary"),
            vmem_limit_bytes=_vmem_limit(blocks, scratch, temps)),
        name="mixer_block",
    )(x3, norm, win, wout, vnorm, ws, bias, convw, gn, rope, ret)


def _rope_tables(seq):
    half = C_HEAD // 2
    inv = ROPE_BASE ** (-np.arange(half, dtype=np.float64) / half)
    ang = np.arange(seq, dtype=np.float64)[:, None] * inv[None, :]
    cos, sin = np.cos(ang), np.sin(ang)
    cc = np.concatenate([cos, cos], axis=-1)
    ss = np.concatenate([-sin, sin], axis=-1)
    scale = C_HEAD ** -0.5
    return jnp.asarray(np.stack([cc, ss, cc * scale, ss * scale]), jnp.float32)


def _retention_tables():
    log_g = np.log1p(-np.exp2(-5.0 - np.arange(C_HEADS, dtype=np.float64)))
    idx = np.arange(CHUNK, dtype=np.float64)
    diff = idx[:, None] - idx[None, :]
    decay = np.where(diff[None] >= 0, np.exp(log_g[:, None, None] * np.maximum(diff, 0.0)[None]), 0.0)
    zeta = np.exp(log_g[:, None] * (CHUNK - 1 - idx)[None, :])
    xi = np.exp(log_g[:, None] * (idx + 1.0)[None, :])
    gamma = np.exp(log_g * CHUNK)
    full = (C_HEADS, CHUNK, CHUNK)
    return jnp.asarray(np.stack([decay,
                                 np.broadcast_to(zeta[:, :, None], full),
                                 np.broadcast_to(xi[:, :, None], full),
                                 np.broadcast_to(gamma[:, None, None], full)]), jnp.float32)


def kernel(x, ffn1_norm, ffn1_w_gate, ffn1_w_up, ffn1_w_down, mix_norm, w_in, gmlp_v_norm, gmlp_w_s,
           gmlp_b_s, conv_w, ret_gn, w_out, ffn2_norm, ffn2_w_gate, ffn2_w_up, ffn2_w_down, final_norm):
    bsz, seq, d = x.shape
    depth = w_in.shape[0]
    assert d == D_MODEL and seq % MIX_TOKENS == 0 and (bsz * seq) % FFN_TOKENS == 0
    bf16 = jnp.bfloat16
    rope = _rope_tables(seq)
    ret = _retention_tables()
    fnorm = final_norm.reshape(1, d)
    ffn1 = (ffn1_norm.reshape(depth, 1, d), ffn1_w_gate.astype(bf16), ffn1_w_up.astype(bf16),
            ffn1_w_down.astype(bf16))
    ffn2 = (ffn2_norm.reshape(depth, 1, d), ffn2_w_gate.astype(bf16), ffn2_w_up.astype(bf16),
            ffn2_w_down.astype(bf16))
    bias = jnp.repeat(jnp.swapaxes(gmlp_b_s, 1, 2), A_HEAD, axis=2)
    mix = (mix_norm.reshape(depth, 1, d), w_in.astype(bf16), w_out.astype(bf16),
           gmlp_v_norm.reshape(depth, 1, A_WIDTH), gmlp_w_s, bias, conv_w, ret_gn.reshape(depth, 1, C_WIDTH))
    x = x.reshape(bsz * seq, d)
    for l in range(depth):
        x = _ffn(x, l, *ffn1, fnorm, apply_final_norm=False)
        x = _mixer(x.reshape(bsz, seq, d), l, *mix, rope, ret).reshape(bsz * seq, d)
        x = _ffn(x, l, *ffn2, fnorm, apply_final_norm=(l == depth - 1))
    return x.reshape(bsz, seq, d)
```

```python
import functools

import jax
import jax.numpy as jnp
import numpy as np
from jax import lax
from jax.experimental import pallas as pl
from jax.experimental.pallas import tpu as pltpu

D_MODEL = 1024
CHUNK = 128
A_GROUPS = 4
A_WIDTH = 256
A_HEAD = A_WIDTH // A_GROUPS
B_WIDTH = 256
CONV_W = 3
C_HEADS = 4
C_WIDTH = 512
C_HEAD = C_WIDTH // C_HEADS
ROPE_BASE = 10000.0
D_FF = 2816
MACARON = 0.5
EPS = 1e-6

A_END = 2 * A_WIDTH
B_END = A_END + 3 * B_WIDTH
D_PROJ = B_END + 4 * C_WIDTH

V7X_VMEM_BYTES = 64 * 1024 * 1024
V7X_MXU_DIM = 256
V7X_SUBLANES = 8

FFN_TOKENS = 1024
FFN_CHUNK = V7X_MXU_DIM
MIX_TOKENS = 512
CONV_HALO = V7X_SUBLANES


def _vmem_limit(block_bytes, scratch_bytes, temp_bytes):
    want = 2 * block_bytes + scratch_bytes + temp_bytes
    return int(min(want, V7X_VMEM_BYTES - 8 * 1024 * 1024))


def _rms(x, gain):
    return x * lax.rsqrt(jnp.mean(x * x, axis=-1, keepdims=True) + EPS) * gain


def _ffn_kernel(x_ref, norm_ref, wg_ref, wu_ref, wd_ref, fnorm_ref, o_ref, *, apply_final_norm):
    x = x_ref[...]
    h = _rms(x, norm_ref[...]).astype(jnp.bfloat16)
    acc = jnp.zeros(x.shape, jnp.float32)
    for c in range(D_FF // FFN_CHUNK):
        cols = slice(c * FFN_CHUNK, (c + 1) * FFN_CHUNK)
        gate = jnp.dot(h, wg_ref[:, cols], preferred_element_type=jnp.float32)
        up = jnp.dot(h, wu_ref[:, cols], preferred_element_type=jnp.float32)
        a = (jax.nn.silu(gate) * up).astype(jnp.bfloat16)
        acc = acc + jnp.dot(a, wd_ref[cols, :], preferred_element_type=jnp.float32)
    y = x + MACARON * acc
    if apply_final_norm:
        y = _rms(y, fnorm_ref[...])
    o_ref[...] = y


def _ffn(x2, layer, norm, wg, wu, wd, fnorm, *, apply_final_norm):
    n_tok = x2.shape[0]
    row = lambda i: (i, 0)
    pick = lambda i: (layer, 0, 0)
    once = pl.Buffered(1)
    tiles = FFN_TOKENS * D_MODEL * 4 * 2
    weights = 3 * D_MODEL * D_FF * 2
    temps = FFN_TOKENS * (D_MODEL * 4 * 3 + 2 * FFN_CHUNK * 4 * 4)
    return pl.pallas_call(
        functools.partial(_ffn_kernel, apply_final_norm=apply_final_norm),
        grid=(n_tok // FFN_TOKENS,),
        in_specs=[
            pl.BlockSpec((FFN_TOKENS, D_MODEL), row),
            pl.BlockSpec((None, 1, D_MODEL), pick),
            pl.BlockSpec((None, D_MODEL, D_FF), pick, pipeline_mode=once),
            pl.BlockSpec((None, D_MODEL, D_FF), pick, pipeline_mode=once),
            pl.BlockSpec((None, D_FF, D_MODEL), pick, pipeline_mode=once),
            pl.BlockSpec((1, D_MODEL), lambda i: (0, 0)),
        ],
        out_specs=pl.BlockSpec((FFN_TOKENS, D_MODEL), row),
        out_shape=jax.ShapeDtypeStruct(x2.shape, jnp.float32),
        compiler_params=pltpu.CompilerParams(
            dimension_semantics=("parallel",),
            vmem_limit_bytes=_vmem_limit(tiles, weights, temps)),
        name="ffn_half_step",
    )(x2, norm, wg, wu, wd, fnorm)


def _split_bf16(v):
    hi = v.astype(jnp.bfloat16)
    lo = (v - hi.astype(jnp.float32)).astype(jnp.bfloat16)
    return hi, lo


def _group_mean(v, avg):
    hi, lo = _split_bf16(v)
    return (jnp.dot(hi, avg, preferred_element_type=jnp.float32)
            + jnp.dot(lo, avg, preferred_element_type=jnp.float32))


def _lane_norm(x, gain):
    mu = jnp.mean(x, axis=-1, keepdims=True)
    d = x - mu
    var = jnp.mean(d * d, axis=-1, keepdims=True)
    return d * lax.rsqrt(var + EPS) * gain


def _mixer_kernel(x_ref, norm_ref, win_ref, wout_ref, vnorm_ref, ws_ref, bias_ref, convw_ref, gn_ref,
                  rope_ref, ret_ref, o_ref, state_ref, zbuf_ref, y_ref):
    f32, bf16 = jnp.float32, jnp.bfloat16
    T = MIX_TOKENS
    first = pl.program_id(1) == 0

    @pl.when(first)
    def _():
        state_ref[...] = jnp.zeros(state_ref.shape, f32)
        zbuf_ref[0:CONV_HALO, :] = jnp.zeros((CONV_HALO, B_WIDTH), f32)

    x = x_ref[...]
    h = _rms(x, norm_ref[...]).astype(bf16)
    n_chunks = T // CHUNK
    col_c = A_WIDTH + B_WIDTH

    pa = jnp.dot(h, win_ref[:, 0:A_END], preferred_element_type=f32)
    pb = jnp.dot(h, win_ref[:, A_END:B_END], preferred_element_type=f32)

    z = jax.nn.gelu(pa, approximate=True)
    u, v = z[:, :A_WIDTH], z[:, A_WIDTH:]
    head_shift = A_HEAD.bit_length() - 1
    lane_grp = lax.shift_right_logical(lax.broadcasted_iota(jnp.int32, (A_WIDTH, A_WIDTH), 1), head_shift)
    row_grp = lax.shift_right_logical(lax.broadcasted_iota(jnp.int32, (A_WIDTH, A_WIDTH), 0), head_shift)
    avg = jnp.where(lane_grp == row_grp, 1.0 / A_HEAD, 0.0).astype(bf16)
    mu = _group_mean(v, avg)
    pqk = jnp.dot(h, win_ref[:, B_END:B_END + 2 * C_WIDTH], preferred_element_type=f32)
    d = v - mu
    var = _group_mean(d * d, avg)
    pvg = jnp.dot(h, win_ref[:, B_END + 2 * C_WIDTH:D_PROJ], preferred_element_type=f32)
    vn = (d * lax.rsqrt(var + EPS) * vnorm_ref[...]).astype(bf16)
    t_idx = lax.broadcasted_iota(jnp.int32, (CHUNK, CHUNK), 0)
    s_idx = lax.broadcasted_iota(jnp.int32, (CHUNK, CHUNK), 1)
    ws = jnp.concatenate([jnp.where(s_idx <= t_idx, ws_ref[g], 0.0) for g in range(A_GROUPS)],
                         axis=1).astype(bf16)
    out_grp = lax.shift_right_logical(lax.broadcasted_iota(jnp.int32, (CHUNK, A_WIDTH), 1), head_shift)
    for n in range(n_chunks):
        rows = slice(n * CHUNK, (n + 1) * CHUNK)
        vc = vn[rows]
        rhs = jnp.concatenate([jnp.where(out_grp == g, vc, jnp.zeros_like(vc)) for g in range(A_GROUPS)],
                              axis=0)
        sv = jnp.dot(ws, rhs, preferred_element_type=f32) + bias_ref[...]
        y_ref[rows, 0:A_WIDTH] = (u[rows] * sv).astype(bf16)

    bg, cg, xin = pb[:, :B_WIDTH], pb[:, B_WIDTH:2 * B_WIDTH], pb[:, 2 * B_WIDTH:]
    zc = cg * xin
    zbuf_ref[CONV_HALO:CONV_HALO + T, :] = zc
    conv = zc * convw_ref[CONV_W - 1:CONV_W, :]
    for i in range(CONV_W - 1):
        lag = CONV_W - 1 - i
        conv = conv + zbuf_ref[CONV_HALO - lag:CONV_HALO - lag + T, :] * convw_ref[i:i + 1, :]
    y_ref[:, A_WIDTH:col_c] = (bg * conv).astype(bf16)
    zbuf_ref[0:CONV_HALO, :] = zbuf_ref[T:T + CONV_HALO, :]

    cos_q, sin_q, cos_k, sin_k = rope_ref[0], rope_ref[1], rope_ref[2], rope_ref[3]
    q_rot, v_bf, scores, kvs = [], [], [], []
    for hd in range(C_HEADS):
        lanes = slice(hd * C_HEAD, (hd + 1) * C_HEAD)
        zeta = ret_ref[1, hd]
        qh = pqk[:, lanes]
        kh = pqk[:, C_WIDTH + hd * C_HEAD:C_WIDTH + (hd + 1) * C_HEAD]
        qh = qh * cos_q + pltpu.roll(qh, C_HEAD // 2, 1) * sin_q
        kh = kh * cos_k + pltpu.roll(kh, C_HEAD // 2, 1) * sin_k
        vb = pvg[:, lanes].astype(bf16)
        q_rot.append(qh)
        v_bf.append(vb)
        for n in range(n_chunks):
            rows = slice(n * CHUNK, (n + 1) * CHUNK)
            kc = kh[rows]
            scores.append(lax.dot_general(qh[rows].astype(bf16), kc.astype(bf16), (((1,), (1,)), ((), ())),
                                          preferred_element_type=f32))
            kvs.append(lax.dot_general((kc * zeta).astype(bf16), vb[rows], (((0,), (0,)), ((), ())),
                                       preferred_element_type=f32))

    out = x + jnp.dot(y_ref[:, 0:col_c], wout_ref[0:col_c, :], preferred_element_type=f32)

    for hd in range(C_HEADS):
        lanes = slice(hd * C_HEAD, (hd + 1) * C_HEAD)
        decay, xi, gamma = ret_ref[0, hd], ret_ref[2, hd], ret_ref[3, hd]
        gate = jax.nn.silu(pvg[:, C_WIDTH + hd * C_HEAD:C_WIDTH + (hd + 1) * C_HEAD])
        gn = gn_ref[:, lanes]
        state = state_ref[hd]
        for n in range(n_chunks):
            rows = slice(n * CHUNK, (n + 1) * CHUNK)
            i = hd * n_chunks + n
            lhs = jnp.concatenate([(scores[i] * decay).astype(bf16), (q_rot[hd][rows] * xi).astype(bf16)],
                                  axis=1)
            rhs = jnp.concatenate([v_bf[hd][rows], state.astype(bf16)], axis=0)
            ret = _lane_norm(jnp.dot(lhs, rhs, preferred_element_type=f32), gn)
            y_ref[rows, col_c + hd * C_HEAD:col_c + (hd + 1) * C_HEAD] = (gate[rows] * ret).astype(bf16)
            state = state * gamma + kvs[i]
        state_ref[hd] = state

    o_ref[...] = out + jnp.dot(y_ref[:, col_c:], wout_ref[col_c:, :], preferred_element_type=f32)


def _mixer(x3, layer, norm, win, wout, vnorm, ws, bias, convw, gn, rope, ret):
    bsz, seq, _ = x3.shape
    tile = lambda b, j: (b, j, 0)
    pick3 = lambda b, j: (layer, 0, 0)
    blocks = (MIX_TOKENS * D_MODEL * 4 * 2 + (D_MODEL * D_PROJ + D_MODEL * D_MODEL) * 2
              + A_GROUPS * CHUNK * CHUNK * 4 + CHUNK * A_WIDTH * 4 + 4 * MIX_TOKENS * C_HEAD * 4 + ret.size * 4)
    scratch = C_HEADS * C_HEAD * C_HEAD * 4 + (CONV_HALO + MIX_TOKENS) * B_WIDTH * 4 + MIX_TOKENS * D_MODEL * 2
    temps = MIX_TOKENS * (D_PROJ * 4 + D_MODEL * 4 * 2) + 4 * 1024 * 1024
    return pl.pallas_call(
        _mixer_kernel,
        grid=(bsz, seq // MIX_TOKENS),
        in_specs=[
            pl.BlockSpec((None, MIX_TOKENS, D_MODEL), tile),
            pl.BlockSpec((None, 1, D_MODEL), pick3),
            pl.BlockSpec((None, D_MODEL, D_PROJ), pick3),
            pl.BlockSpec((None, D_MODEL, D_MODEL), pick3),
            pl.BlockSpec((None, 1, A_WIDTH), pick3),
            pl.BlockSpec((None, A_GROUPS, CHUNK, CHUNK), lambda b, j: (layer, 0, 0, 0)),
            pl.BlockSpec((None, CHUNK, A_WIDTH), pick3),
            pl.BlockSpec((None, CONV_W, B_WIDTH), pick3),
            pl.BlockSpec((None, 1, C_WIDTH), pick3),
            pl.BlockSpec((4, MIX_TOKENS, C_HEAD), lambda b, j: (0, j, 0)),
            pl.BlockSpec(ret.shape, lambda b, j: (0, 0, 0, 0)),
        ],
        out_specs=pl.BlockSpec((None, MIX_TOKENS, D_MODEL), tile),
        out_shape=jax.ShapeDtypeStruct(x3.shape, jnp.float32),
        scratch_shapes=[
            pltpu.VMEM((C_HEADS, C_HEAD, C_HEAD), jnp.float32),
            pltpu.VMEM((CONV_HALO + MIX_TOKENS, B_WIDTH), jnp.float32),
            pltpu.VMEM((MIX_TOKENS, D_MODEL), jnp.bfloat16),
        ],
        compiler_params=pltpu.CompilerParams(
            dimension_semantics=("parallel", "arbitrary"),
            vmem_limit_bytes=_vmem_limit(blocks, scratch, temps)),
        name="mixer_block",
    )(x3, norm, win, wout, vnorm, ws, bias, convw, gn, rope, ret)


def _rope_tables(seq):
    half = C_HEAD // 2
    inv = ROPE_BASE ** (-np.arange(half, dtype=np.float64) / half)
    ang = np.arange(seq, dtype=np.float64)[:, None] * inv[None, :]
    cos, sin = np.cos(ang), np.sin(ang)
    cc = np.concatenate([cos, cos], axis=-1)
    ss = np.concatenate([-sin, sin], axis=-1)
    scale = C_HEAD ** -0.5
    return jnp.asarray(np.stack([cc, ss, cc * scale, ss * scale]), jnp.float32)


def _retention_tables():
    log_g = np.log1p(-np.exp2(-5.0 - np.arange(C_HEADS, dtype=np.float64)))
    idx = np.arange(CHUNK, dtype=np.float64)
    diff = idx[:, None] - idx[None, :]
    decay = np.where(diff[None] >= 0, np.exp(log_g[:, None, None] * np.maximum(diff, 0.0)[None]), 0.0)
    zeta = np.exp(log_g[:, None] * (CHUNK - 1 - idx)[None, :])
    xi = np.exp(log_g[:, None] * (idx + 1.0)[None, :])
    gamma = np.exp(log_g * CHUNK)
    full = (C_HEADS, CHUNK, CHUNK)
    return jnp.asarray(np.stack([decay,
                                 np.broadcast_to(zeta[:, :, None], full),
                                 np.broadcast_to(xi[:, :, None], full),
                                 np.broadcast_to(gamma[:, None, None], full)]), jnp.float32)


def kernel(x, ffn1_norm, ffn1_w_gate, ffn1_w_up, ffn1_w_down, mix_norm, w_in, gmlp_v_norm, gmlp_w_s,
           gmlp_b_s, conv_w, ret_gn, w_out, ffn2_norm, ffn2_w_gate, ffn2_w_up, ffn2_w_down, final_norm):
    bsz, seq, d = x.shape
    depth = w_in.shape[0]
    assert d == D_MODEL and seq % MIX_TOKENS == 0 and (bsz * seq) % FFN_TOKENS == 0
    bf16 = jnp.bfloat16
    rope = _rope_tables(seq)
    ret = _retention_tables()
    fnorm = final_norm.reshape(1, d)
    ffn1 = (ffn1_norm.reshape(depth, 1, d), ffn1_w_gate.astype(bf16), ffn1_w_up.astype(bf16),
            ffn1_w_down.astype(bf16))
    ffn2 = (ffn2_norm.reshape(depth, 1, d), ffn2_w_gate.astype(bf16), ffn2_w_up.astype(bf16),
            ffn2_w_down.astype(bf16))
    bias = jnp.repeat(jnp.swapaxes(gmlp_b_s, 1, 2), A_HEAD, axis=2)
    mix = (mix_norm.reshape(depth, 1, d), w_in.astype(bf16), w_out.astype(bf16),
           gmlp_v_norm.reshape(depth, 1, A_WIDTH), gmlp_w_s, bias, conv_w, ret_gn.reshape(depth, 1, C_WIDTH))
    x = x.reshape(bsz * seq, d)
    for l in range(depth):
        x = _ffn(x, l, *ffn1, fnorm, apply_final_norm=False)
        x = _mixer(x.reshape(bsz, seq, d), l, *mix, rope, ret).reshape(bsz * seq, d)
        x = _ffn(x, l, *ffn2, fnorm, apply_final_norm=(l == depth - 1))
    return x.reshape(bsz, seq, d)
```

```python
import functools

import jax
import jax.numpy as jnp
import numpy as np
from jax import lax
from jax.experimental import pallas as pl
from jax.experimental.pallas import tpu as pltpu

D_MODEL = 1024
CHUNK = 128
A_GROUPS = 4
A_WIDTH = 256
A_HEAD = A_WIDTH // A_GROUPS
B_WIDTH = 256
CONV_W = 3
C_HEADS = 4
C_WIDTH = 512
C_HEAD = C_WIDTH // C_HEADS
ROPE_BASE = 10000.0
D_FF = 2816
MACARON = 0.5
EPS = 1e-6

A_END = 2 * A_WIDTH
B_END = A_END + 3 * B_WIDTH
D_PROJ = B_END + 4 * C_WIDTH

V7X_VMEM_BYTES = 64 * 1024 * 1024
V7X_MXU_DIM = 256
V7X_SUBLANES = 8

FFN_TOKENS = 1024
FFN_CHUNK = V7X_MXU_DIM
MIX_TOKENS = 1024
CONV_HALO = V7X_SUBLANES


def _vmem_limit(block_bytes, scratch_bytes, temp_bytes):
    want = 2 * block_bytes + scratch_bytes + temp_bytes
    return int(min(want, V7X_VMEM_BYTES - 8 * 1024 * 1024))


def _rms(x, gain):
    return x * lax.rsqrt(jnp.mean(x * x, axis=-1, keepdims=True) + EPS) * gain


def _ffn_kernel(x_ref, norm_ref, wg_ref, wu_ref, wd_ref, fnorm_ref, o_ref, *, apply_final_norm):
    x = x_ref[...]
    h = _rms(x, norm_ref[...]).astype(jnp.bfloat16)
    acc = jnp.zeros(x.shape, jnp.float32)
    for c in range(D_FF // FFN_CHUNK):
        cols = slice(c * FFN_CHUNK, (c + 1) * FFN_CHUNK)
        gate = jnp.dot(h, wg_ref[:, cols], preferred_element_type=jnp.float32)
        up = jnp.dot(h, wu_ref[:, cols], preferred_element_type=jnp.float32)
        a = (jax.nn.silu(gate) * up).astype(jnp.bfloat16)
        acc = acc + jnp.dot(a, wd_ref[cols, :], preferred_element_type=jnp.float32)
    y = x + MACARON * acc
    if apply_final_norm:
        y = _rms(y, fnorm_ref[...])
    o_ref[...] = y


def _ffn(x2, layer, norm, wg, wu, wd, fnorm, *, apply_final_norm):
    n_tok = x2.shape[0]
    row = lambda i: (i, 0)
    pick = lambda i: (layer, 0, 0)
    once = pl.Buffered(1)
    tiles = FFN_TOKENS * D_MODEL * 4 * 2
    weights = 3 * D_MODEL * D_FF * 2
    temps = FFN_TOKENS * (D_MODEL * 4 * 3 + 2 * FFN_CHUNK * 4 * 4)
    return pl.pallas_call(
        functools.partial(_ffn_kernel, apply_final_norm=apply_final_norm),
        grid=(n_tok // FFN_TOKENS,),
        in_specs=[
            pl.BlockSpec((FFN_TOKENS, D_MODEL), row),
            pl.BlockSpec((None, 1, D_MODEL), pick),
            pl.BlockSpec((None, D_MODEL, D_FF), pick, pipeline_mode=once),
            pl.BlockSpec((None, D_MODEL, D_FF), pick, pipeline_mode=once),
            pl.BlockSpec((None, D_FF, D_MODEL), pick, pipeline_mode=once),
            pl.BlockSpec((1, D_MODEL), lambda i: (0, 0)),
        ],
        out_specs=pl.BlockSpec((FFN_TOKENS, D_MODEL), row),
        out_shape=jax.ShapeDtypeStruct(x2.shape, jnp.float32),
        compiler_params=pltpu.CompilerParams(
            dimension_semantics=("parallel",),
            vmem_limit_bytes=_vmem_limit(tiles, weights, temps)),
        name="ffn_half_step",
    )(x2, norm, wg, wu, wd, fnorm)


def _split_bf16(v):
    hi = v.astype(jnp.bfloat16)
    lo = (v - hi.astype(jnp.float32)).astype(jnp.bfloat16)
    return hi, lo


def _group_mean(v, avg):
    hi, lo = _split_bf16(v)
    return (jnp.dot(hi, avg, preferred_element_type=jnp.float32)
            + jnp.dot(lo, avg, preferred_element_type=jnp.float32))


def _lane_norm(x, gain):
    mu = jnp.mean(x, axis=-1, keepdims=True)
    d = x - mu
    var = jnp.mean(d * d, axis=-1, keepdims=True)
    return d * lax.rsqrt(var + EPS) * gain


def _mixer_kernel(x_ref, norm_ref, win_ref, wout_ref, vnorm_ref, ws_ref, bias_ref, convw_ref, gn_ref,
                  rope_ref, ret_ref, o_ref, state_ref, zbuf_ref, y_ref):
    f32, bf16 = jnp.float32, jnp.bfloat16
    T = MIX_TOKENS
    first = pl.program_id(1) == 0

    @pl.when(first)
    def _():
        state_ref[...] = jnp.zeros(state_ref.shape, f32)
        zbuf_ref[0:CONV_HALO, :] = jnp.zeros((CONV_HALO, B_WIDTH), f32)

    x = x_ref[...]
    h = _rms(x, norm_ref[...]).astype(bf16)
    n_chunks = T // CHUNK
    col_c = A_WIDTH + B_WIDTH

    pa = jnp.dot(h, win_ref[:, 0:A_END], preferred_element_type=f32)
    pb = jnp.dot(h, win_ref[:, A_END:B_END], preferred_element_type=f32)

    z = jax.nn.gelu(pa, approximate=True)
    u, v = z[:, :A_WIDTH], z[:, A_WIDTH:]
    head_shift = A_HEAD.bit_length() - 1
    lane_grp = lax.shift_right_logical(lax.broadcasted_iota(jnp.int32, (A_WIDTH, A_WIDTH), 1), head_shift)
    row_grp = lax.shift_right_logical(lax.broadcasted_iota(jnp.int32, (A_WIDTH, A_WIDTH), 0), head_shift)
    avg = jnp.where(lane_grp == row_grp, 1.0 / A_HEAD, 0.0).astype(bf16)
    mu = _group_mean(v, avg)
    pqk = jnp.dot(h, win_ref[:, B_END:B_END + 2 * C_WIDTH], preferred_element_type=f32)
    d = v - mu
    var = _group_mean(d * d, avg)
    pv = jnp.dot(h, win_ref[:, B_END + 2 * C_WIDTH:B_END + 3 * C_WIDTH], preferred_element_type=f32)
    vn = (d * lax.rsqrt(var + EPS) * vnorm_ref[...]).astype(bf16)
    t_idx = lax.broadcasted_iota(jnp.int32, (CHUNK, CHUNK), 0)
    s_idx = lax.broadcasted_iota(jnp.int32, (CHUNK, CHUNK), 1)
    ws = jnp.concatenate([jnp.where(s_idx <= t_idx, ws_ref[g], 0.0) for g in range(A_GROUPS)],
                         axis=1).astype(bf16)
    out_grp = lax.shift_right_logical(lax.broadcasted_iota(jnp.int32, (CHUNK, A_WIDTH), 1), head_shift)
    for n in range(n_chunks):
        rows = slice(n * CHUNK, (n + 1) * CHUNK)
        vc = vn[rows]
        rhs = jnp.concatenate([jnp.where(out_grp == g, vc, jnp.zeros_like(vc)) for g in range(A_GROUPS)],
                              axis=0)
        sv = jnp.dot(ws, rhs, preferred_element_type=f32) + bias_ref[...]
        y_ref[rows, 0:A_WIDTH] = (u[rows] * sv).astype(bf16)

    bg, cg, xin = pb[:, :B_WIDTH], pb[:, B_WIDTH:2 * B_WIDTH], pb[:, 2 * B_WIDTH:]
    zc = cg * xin
    zbuf_ref[CONV_HALO:CONV_HALO + T, :] = zc
    conv = zc * convw_ref[CONV_W - 1:CONV_W, :]
    for i in range(CONV_W - 1):
        lag = CONV_W - 1 - i
        conv = conv + zbuf_ref[CONV_HALO - lag:CONV_HALO - lag + T, :] * convw_ref[i:i + 1, :]
    y_ref[:, A_WIDTH:col_c] = (bg * conv).astype(bf16)
    zbuf_ref[0:CONV_HALO, :] = zbuf_ref[T:T + CONV_HALO, :]

    cos, sin = rope_ref[0], rope_ref[1]
    q_rot, v_bf, scores, kvs = [], [], [], []
    for hd in range(C_HEADS):
        lanes = slice(hd * C_HEAD, (hd + 1) * C_HEAD)
        zeta = ret_ref[1, hd]
        qh = pqk[:, lanes]
        kh = pqk[:, C_WIDTH + hd * C_HEAD:C_WIDTH + (hd + 1) * C_HEAD]
        qh = qh * cos + pltpu.roll(qh, C_HEAD // 2, 1) * sin
        kh = kh * cos + pltpu.roll(kh, C_HEAD // 2, 1) * sin
        vb = pv[:, lanes].astype(bf16)
        q_rot.append(qh)
        v_bf.append(vb)
        for n in range(n_chunks):
            rows = slice(n * CHUNK, (n + 1) * CHUNK)
            kc = kh[rows]
            scores.append(lax.dot_general(qh[rows].astype(bf16), kc.astype(bf16), (((1,), (1,)), ((), ())),
                                          preferred_element_type=f32))
            kvs.append(lax.dot_general((kc * zeta).astype(bf16), vb[rows], (((0,), (0,)), ((), ())),
                                       preferred_element_type=f32))

    pg = jnp.dot(h, win_ref[:, B_END + 3 * C_WIDTH:D_PROJ], preferred_element_type=f32)

    rets = []
    for hd in range(C_HEADS):
        decay, xi, gamma = ret_ref[0, hd], ret_ref[2, hd], ret_ref[3, hd]
        state = state_ref[hd]
        for n in range(n_chunks):
            rows = slice(n * CHUNK, (n + 1) * CHUNK)
            i = hd * n_chunks + n
            lhs = jnp.concatenate([(scores[i] * decay).astype(bf16), (q_rot[hd][rows] * xi).astype(bf16)],
                                  axis=1)
            rhs = jnp.concatenate([v_bf[hd][rows], state.astype(bf16)], axis=0)
            rets.append(jnp.dot(lhs, rhs, preferred_element_type=f32))
            state = state * gamma + kvs[i]
        state_ref[hd] = state

    out = x + jnp.dot(y_ref[:, 0:col_c], wout_ref[0:col_c, :], preferred_element_type=f32)

    for hd in range(C_HEADS):
        lanes = slice(hd * C_HEAD, (hd + 1) * C_HEAD)
        gate = jax.nn.silu(pg[:, lanes])
        gn = gn_ref[:, lanes]
        for n in range(n_chunks):
            rows = slice(n * CHUNK, (n + 1) * CHUNK)
            ret = _lane_norm(rets[hd * n_chunks + n], gn)
            y_ref[rows, col_c + hd * C_HEAD:col_c + (hd + 1) * C_HEAD] = (gate[rows] * ret).astype(bf16)

    o_ref[...] = out + jnp.dot(y_ref[:, col_c:], wout_ref[col_c:, :], preferred_element_type=f32)


def _mixer(x3, layer, norm, win, wout, vnorm, ws, bias, convw, gn, rope, ret):
    bsz, seq, _ = x3.shape
    tile = lambda b, j: (b, j, 0)
    pick3 = lambda b, j: (layer, 0, 0)
    once = pl.Buffered(1)
    blocks = (MIX_TOKENS * D_MODEL * 4 * 2 + A_GROUPS * CHUNK * CHUNK * 4 + CHUNK * A_WIDTH * 4
              + 2 * MIX_TOKENS * C_HEAD * 4 + ret.size * 4)
    weights = (D_MODEL * D_PROJ + D_MODEL * D_MODEL) * 2
    scratch = C_HEADS * C_HEAD * C_HEAD * 4 + (CONV_HALO + MIX_TOKENS) * B_WIDTH * 4 + MIX_TOKENS * D_MODEL * 2
    temps = MIX_TOKENS * (D_PROJ * 4 + D_MODEL * 4 * 2) + 4 * 1024 * 1024
    return pl.pallas_call(
        _mixer_kernel,
        grid=(bsz, seq // MIX_TOKENS),
        in_specs=[
            pl.BlockSpec((None, MIX_TOKENS, D_MODEL), tile),
            pl.BlockSpec((None, 1, D_MODEL), pick3),
            pl.BlockSpec((None, D_MODEL, D_PROJ), pick3, pipeline_mode=once),
            pl.BlockSpec((None, D_MODEL, D_MODEL), pick3, pipeline_mode=once),
            pl.BlockSpec((None, 1, A_WIDTH), pick3),
            pl.BlockSpec((None, A_GROUPS, CHUNK, CHUNK), lambda b, j: (layer, 0, 0, 0)),
            pl.BlockSpec((None, CHUNK, A_WIDTH), pick3),
            pl.BlockSpec((None, CONV_W, B_WIDTH), pick3),
            pl.BlockSpec((None, 1, C_WIDTH), pick3),
            pl.BlockSpec((2, MIX_TOKENS, C_HEAD), lambda b, j: (0, j, 0)),
            pl.BlockSpec(ret.shape, lambda b, j: (0, 0, 0, 0)),
        ],
        out_specs=pl.BlockSpec((None, MIX_TOKENS, D_MODEL), tile),
        out_shape=jax.ShapeDtypeStruct(x3.shape, jnp.float32),
        scratch_shapes=[
            pltpu.VMEM((C_HEADS, C_HEAD, C_HEAD), jnp.float32),
            pltpu.VMEM((CONV_HALO + MIX_TOKENS, B_WIDTH), jnp.float32),
            pltpu.VMEM((MIX_TOKENS, D_MODEL), jnp.bfloat16),
        ],
        compiler_params=pltpu.CompilerParams(
            dimension_semantics=("parallel", "arbitrary"),
            vmem_limit_bytes=_vmem_limit(blocks, weights + scratch, temps)),
        name="mixer_block",
    )(x3, norm, win, wout, vnorm, ws, bias, convw, gn, rope, ret)


def _rope_tables(seq):
    half = C_HEAD // 2
    inv = ROPE_BASE ** (-np.arange(half, dtype=np.float64) / half)
    ang = np.arange(seq, dtype=np.float64)[:, None] * inv[None, :]
    cos, sin = np.cos(ang), np.sin(ang)
    cc = np.concatenate([cos, cos], axis=-1)
    ss = np.concatenate([-sin, sin], axis=-1)
    return jnp.asarray(np.stack([cc, ss]), jnp.float32)


def _retention_tables():
    log_g = np.log1p(-np.exp2(-5.0 - np.arange(C_HEADS, dtype=np.float64)))
    idx = np.arange(CHUNK, dtype=np.float64)
    diff = idx[:, None] - idx[None, :]
    scale = C_HEAD ** -0.5
    decay = scale * np.where(diff[None] >= 0, np.exp(log_g[:, None, None] * np.maximum(diff, 0.0)[None]), 0.0)
    zeta = scale * np.exp(log_g[:, None] * (CHUNK - 1 - idx)[None, :])
    xi = np.exp(log_g[:, None] * (idx + 1.0)[None, :])
    gamma = np.exp(log_g * CHUNK)
    full = (C_HEADS, CHUNK, CHUNK)
    return jnp.asarray(np.stack([decay,
                                 np.broadcast_to(zeta[:, :, None], full),
                                 np.broadcast_to(xi[:, :, None], full),
                                 np.broadcast_to(gamma[:, None, None], full)]), jnp.float32)


def kernel(x, ffn1_norm, ffn1_w_gate, ffn1_w_up, ffn1_w_down, mix_norm, w_in, gmlp_v_norm, gmlp_w_s,
           gmlp_b_s, conv_w, ret_gn, w_out, ffn2_norm, ffn2_w_gate, ffn2_w_up, ffn2_w_down, final_norm):
    bsz, seq, d = x.shape
    depth = w_in.shape[0]
    assert d == D_MODEL and seq % MIX_TOKENS == 0 and (bsz * seq) % FFN_TOKENS == 0
    bf16 = jnp.bfloat16
    rope = _rope_tables(seq)
    ret = _retention_tables()
    fnorm = final_norm.reshape(1, d)
    ffn1 = (ffn1_norm.reshape(depth, 1, d), ffn1_w_gate.astype(bf16), ffn1_w_up.astype(bf16),
            ffn1_w_down.astype(bf16))
    ffn2 = (ffn2_norm.reshape(depth, 1, d), ffn2_w_gate.astype(bf16), ffn2_w_up.astype(bf16),
            ffn2_w_down.astype(bf16))
    bias = jnp.repeat(jnp.swapaxes(gmlp_b_s, 1, 2), A_HEAD, axis=2)
    mix = (mix_norm.reshape(depth, 1, d), w_in.astype(bf16), w_out.astype(bf16),
           gmlp_v_norm.reshape(depth, 1, A_WIDTH), gmlp_w_s, bias, conv_w, ret_gn.reshape(depth, 1, C_WIDTH))
    x = x.reshape(bsz * seq, d)
    for l in range(depth):
        x = _ffn(x, l, *ffn1, fnorm, apply_final_norm=False)
        x = _mixer(x.reshape(bsz, seq, d), l, *mix, rope, ret).reshape(bsz * seq, d)
        x = _ffn(x, l, *ffn2, fnorm, apply_final_norm=(l == depth - 1))
    return x.reshape(bsz, seq, d)
```

```python
import functools

import jax
import jax.numpy as jnp
import numpy as np
from jax import lax
from jax.experimental import pallas as pl
from jax.experimental.pallas import tpu as pltpu

D_MODEL = 1024
CHUNK = 128
A_GROUPS = 4
A_WIDTH = 256
A_HEAD = A_WIDTH // A_GROUPS
B_WIDTH = 256
CONV_W = 3
C_HEADS = 4
C_WIDTH = 512
C_HEAD = C_WIDTH // C_HEADS
ROPE_BASE = 10000.0
D_FF = 2816
MACARON = 0.5
EPS = 1e-6

A_END = 2 * A_WIDTH
B_END = A_END + 3 * B_WIDTH
D_PROJ = B_END + 4 * C_WIDTH

V7X_VMEM_BYTES = 64 * 1024 * 1024
V7X_MXU_DIM = 256
V7X_SUBLANES = 8
V7X_BF16_SUBLANES = 2 * V7X_SUBLANES

FFN_TOKENS = 1024
FFN_CHUNK = V7X_MXU_DIM
MIX_TOKENS = 1024
CONV_HALO = V7X_SUBLANES


def _vmem_limit(block_bytes, scratch_bytes, temp_bytes):
    want = 2 * block_bytes + scratch_bytes + temp_bytes
    return int(min(want, V7X_VMEM_BYTES - 8 * 1024 * 1024))


def _rms(x, gain):
    return x * lax.rsqrt(jnp.mean(x * x, axis=-1, keepdims=True) + EPS) * gain


def _cast_specs(jobs, n_steps, step_of):
    in_specs, out_specs, out_shapes, nbytes = [], [], [], 0
    for arr, layer in jobs:
        _, rows, cols = arr.shape
        n_blocks = n_steps
        while rows % n_blocks or (rows // n_blocks) % V7X_BF16_SUBLANES:
            n_blocks //= 2
        br = rows // n_blocks

        def src_map(*idx, layer=layer, last=n_blocks - 1):
            return (layer, jnp.minimum(step_of(*idx), last), 0)

        def dst_map(*idx, last=n_blocks - 1):
            return (jnp.minimum(step_of(*idx), last), 0)

        in_specs.append(pl.BlockSpec((None, br, cols), src_map))
        out_specs.append(pl.BlockSpec((br, cols), dst_map))
        out_shapes.append(jax.ShapeDtypeStruct((rows, cols), jnp.bfloat16))
        nbytes += br * cols * (4 + 2)
    return in_specs, out_specs, out_shapes, nbytes


def _cast_blocks(src_refs, dst_refs):
    for src, dst in zip(src_refs, dst_refs):
        dst[...] = src[...].astype(jnp.bfloat16)


def _ffn_kernel(*refs, n_cast, apply_final_norm):
    x_ref, norm_ref, wg_ref, wu_ref, wd_ref, fnorm_ref = refs[:6]
    o_ref = refs[6 + n_cast]
    x = x_ref[...]
    h = _rms(x, norm_ref[...]).astype(jnp.bfloat16)
    acc = jnp.zeros(x.shape, jnp.float32)
    for c in range(D_FF // FFN_CHUNK):
        cols = slice(c * FFN_CHUNK, (c + 1) * FFN_CHUNK)
        gate = jnp.dot(h, wg_ref[:, cols], preferred_element_type=jnp.float32)
        up = jnp.dot(h, wu_ref[:, cols], preferred_element_type=jnp.float32)
        if c == 0:
            _cast_blocks(refs[6:6 + n_cast], refs[7 + n_cast:])
        a = (jax.nn.silu(gate) * up).astype(jnp.bfloat16)
        acc = acc + jnp.dot(a, wd_ref[cols, :], preferred_element_type=jnp.float32)
    y = x + MACARON * acc
    if apply_final_norm:
        y = _rms(y, fnorm_ref[...])
    o_ref[...] = y


def _ffn(x2, layer, norm, weights, fnorm, cast_jobs, *, apply_final_norm):
    n_tok = x2.shape[0]
    n_steps = n_tok // FFN_TOKENS
    row = lambda i: (i, 0)
    fixed = lambda i: (0, 0)
    once = pl.Buffered(1)
    c_in, c_out, c_shapes, c_bytes = _cast_specs(cast_jobs, n_steps, lambda i: i)
    tiles = FFN_TOKENS * D_MODEL * 4 * 2 + c_bytes
    weight_bytes = 3 * D_MODEL * D_FF * 2
    temps = FFN_TOKENS * (D_MODEL * 4 * 3 + 2 * FFN_CHUNK * 4 * 4)
    outs = pl.pallas_call(
        functools.partial(_ffn_kernel, n_cast=len(cast_jobs), apply_final_norm=apply_final_norm),
        grid=(n_steps,),
        in_specs=[
            pl.BlockSpec((FFN_TOKENS, D_MODEL), row),
            pl.BlockSpec((None, 1, D_MODEL), lambda i: (layer, 0, 0)),
            pl.BlockSpec((D_MODEL, D_FF), fixed, pipeline_mode=once),
            pl.BlockSpec((D_MODEL, D_FF), fixed, pipeline_mode=once),
            pl.BlockSpec((D_FF, D_MODEL), fixed, pipeline_mode=once),
            pl.BlockSpec((1, D_MODEL), fixed),
        ] + c_in,
        out_specs=[pl.BlockSpec((FFN_TOKENS, D_MODEL), row)] + c_out,
        out_shape=[jax.ShapeDtypeStruct(x2.shape, jnp.float32)] + c_shapes,
        compiler_params=pltpu.CompilerParams(
            dimension_semantics=("arbitrary",),
            vmem_limit_bytes=_vmem_limit(tiles, weight_bytes, temps)),
        name="ffn_half_step",
    )(x2, norm, *weights, fnorm, *[a for a, _ in cast_jobs])
    return outs[0], outs[1:]


def _split_bf16(v):
    hi = v.astype(jnp.bfloat16)
    lo = (v - hi.astype(jnp.float32)).astype(jnp.bfloat16)
    return hi, lo


def _group_mean(v, avg):
    hi, lo = _split_bf16(v)
    return (jnp.dot(hi, avg, preferred_element_type=jnp.float32)
            + jnp.dot(lo, avg, preferred_element_type=jnp.float32))


def _lane_norm(x, gain):
    mu = jnp.mean(x, axis=-1, keepdims=True)
    d = x - mu
    var = jnp.mean(d * d, axis=-1, keepdims=True)
    return d * lax.rsqrt(var + EPS) * gain


N_MIXER_INPUTS = 11


def _mixer_kernel(*refs, n_cast):
    (x_ref, norm_ref, win_ref, wout_ref, vnorm_ref, ws_ref, bias_ref, convw_ref, gn_ref, rope_ref,
     ret_ref) = refs[:N_MIXER_INPUTS]
    o_ref = refs[N_MIXER_INPUTS + n_cast]
    state_ref, zbuf_ref, y_ref = refs[N_MIXER_INPUTS + 2 * n_cast + 1:]
    f32, bf16 = jnp.float32, jnp.bfloat16
    T = MIX_TOKENS
    first = pl.program_id(1) == 0

    @pl.when(first)
    def _():
        state_ref[...] = jnp.zeros(state_ref.shape, f32)
        zbuf_ref[0:CONV_HALO, :] = jnp.zeros((CONV_HALO, B_WIDTH), f32)

    x = x_ref[...]
    h = _rms(x, norm_ref[...]).astype(bf16)
    n_chunks = T // CHUNK
    col_c = A_WIDTH + B_WIDTH

    pa = jnp.dot(h, win_ref[:, 0:A_END], preferred_element_type=f32)
    pb = jnp.dot(h, win_ref[:, A_END:B_END], preferred_element_type=f32)
    _cast_blocks(refs[N_MIXER_INPUTS:N_MIXER_INPUTS + n_cast],
                 refs[N_MIXER_INPUTS + n_cast + 1:N_MIXER_INPUTS + 2 * n_cast + 1])

    z = jax.nn.gelu(pa, approximate=True)
    u, v = z[:, :A_WIDTH], z[:, A_WIDTH:]
    head_shift = A_HEAD.bit_length() - 1
    lane_grp = lax.shift_right_logical(lax.broadcasted_iota(jnp.int32, (A_WIDTH, A_WIDTH), 1), head_shift)
    row_grp = lax.shift_right_logical(lax.broadcasted_iota(jnp.int32, (A_WIDTH, A_WIDTH), 0), head_shift)
    avg = jnp.where(lane_grp == row_grp, 1.0 / A_HEAD, 0.0).astype(bf16)
    mu = _group_mean(v, avg)
    pqk = jnp.dot(h, win_ref[:, B_END:B_END + 2 * C_WIDTH], preferred_element_type=f32)
    d = v - mu
    var = _group_mean(d * d, avg)
    pv = jnp.dot(h, win_ref[:, B_END + 2 * C_WIDTH:B_END + 3 * C_WIDTH], preferred_element_type=f32)
    vn = (d * lax.rsqrt(var + EPS) * vnorm_ref[...]).astype(bf16)
    t_idx = lax.broadcasted_iota(jnp.int32, (CHUNK, CHUNK), 0)
    s_idx = lax.broadcasted_iota(jnp.int32, (CHUNK, CHUNK), 1)
    ws = jnp.concatenate([jnp.where(s_idx <= t_idx, ws_ref[g], 0.0) for g in range(A_GROUPS)],
                         axis=1).astype(bf16)
    out_grp = lax.shift_right_logical(lax.broadcasted_iota(jnp.int32, (CHUNK, A_WIDTH), 1), head_shift)
    for n in range(n_chunks):
        rows = slice(n * CHUNK, (n + 1) * CHUNK)
        vc = vn[rows]
        rhs = jnp.concatenate([jnp.where(out_grp == g, vc, jnp.zeros_like(vc)) for g in range(A_GROUPS)],
                              axis=0)
        sv = jnp.dot(ws, rhs, preferred_element_type=f32) + bias_ref[...]
        y_ref[rows, 0:A_WIDTH] = (u[rows] * sv).astype(bf16)

    bg, cg, xin = pb[:, :B_WIDTH], pb[:, B_WIDTH:2 * B_WIDTH], pb[:, 2 * B_WIDTH:]
    zc = cg * xin
    zbuf_ref[CONV_HALO:CONV_HALO + T, :] = zc
    conv = zc * convw_ref[CONV_W - 1:CONV_W, :]
    for i in range(CONV_W - 1):
        lag = CONV_W - 1 - i
        conv = conv + zbuf_ref[CONV_HALO - lag:CONV_HALO - lag + T, :] * convw_ref[i:i + 1, :]
    y_ref[:, A_WIDTH:col_c] = (bg * conv).astype(bf16)
    zbuf_ref[0:CONV_HALO, :] = zbuf_ref[T:T + CONV_HALO, :]

    cos, sin = rope_ref[0], rope_ref[1]
    q_rot, v_bf, scores, kvs = [], [], [], []
    for hd in range(C_HEADS):
        lanes = slice(hd * C_HEAD, (hd + 1) * C_HEAD)
        zeta = ret_ref[1, hd]
        qh = pqk[:, lanes]
        kh = pqk[:, C_WIDTH + hd * C_HEAD:C_WIDTH + (hd + 1) * C_HEAD]
        qh = qh * cos + pltpu.roll(qh, C_HEAD // 2, 1) * sin
        kh = kh * cos + pltpu.roll(kh, C_HEAD // 2, 1) * sin
        vb = pv[:, lanes].astype(bf16)
        q_rot.append(qh)
        v_bf.append(vb)
        for n in range(n_chunks):
            rows = slice(n * CHUNK, (n + 1) * CHUNK)
            kc = kh[rows]
            scores.append(lax.dot_general(qh[rows].astype(bf16), kc.astype(bf16), (((1,), (1,)), ((), ())),
                                          preferred_element_type=f32))
            kvs.append(lax.dot_general((kc * zeta).astype(bf16), vb[rows], (((0,), (0,)), ((), ())),
                                       preferred_element_type=f32))

    pg = jnp.dot(h, win_ref[:, B_END + 3 * C_WIDTH:D_PROJ], preferred_element_type=f32)

    rets = []
    for hd in range(C_HEADS):
        decay, xi, gamma = ret_ref[0, hd], ret_ref[2, hd], ret_ref[3, hd]
        state = state_ref[hd]
        for n in range(n_chunks):
            rows = slice(n * CHUNK, (n + 1) * CHUNK)
            i = hd * n_chunks + n
            lhs = jnp.concatenate([(scores[i] * decay).astype(bf16), (q_rot[hd][rows] * xi).astype(bf16)],
                                  axis=1)
            rhs = jnp.concatenate([v_bf[hd][rows], state.astype(bf16)], axis=0)
            rets.append(jnp.dot(lhs, rhs, preferred_element_type=f32))
            state = state * gamma + kvs[i]
        state_ref[hd] = state

    out = x + jnp.dot(y_ref[:, 0:col_c], wout_ref[0:col_c, :], preferred_element_type=f32)

    for hd in range(C_HEADS):
        lanes = slice(hd * C_HEAD, (hd + 1) * C_HEAD)
        gate = jax.nn.silu(pg[:, lanes])
        gn = gn_ref[:, lanes]
        for n in range(n_chunks):
            rows = slice(n * CHUNK, (n + 1) * CHUNK)
            ret = _lane_norm(rets[hd * n_chunks + n], gn)
            y_ref[rows, col_c + hd * C_HEAD:col_c + (hd + 1) * C_HEAD] = (gate[rows] * ret).astype(bf16)

    o_ref[...] = out + jnp.dot(y_ref[:, col_c:], wout_ref[col_c:, :], preferred_element_type=f32)


def _mixer(x3, layer, norm, weights, vnorm, ws, bias, convw, gn, rope, ret, cast_jobs):
    bsz, seq, _ = x3.shape
    seq_steps = seq // MIX_TOKENS
    tile = lambda b, j: (b, j, 0)
    pick3 = lambda b, j: (layer, 0, 0)
    fixed = lambda b, j: (0, 0)
    once = pl.Buffered(1)
    c_in, c_out, c_shapes, c_bytes = _cast_specs(cast_jobs, bsz * seq_steps, lambda b, j: b * seq_steps + j)
    blocks = (MIX_TOKENS * D_MODEL * 4 * 2 + A_GROUPS * CHUNK * CHUNK * 4 + CHUNK * A_WIDTH * 4
              + 2 * MIX_TOKENS * C_HEAD * 4 + ret.size * 4 + c_bytes)
    weight_bytes = (D_MODEL * D_PROJ + D_MODEL * D_MODEL) * 2
    scratch = C_HEADS * C_HEAD * C_HEAD * 4 + (CONV_HALO + MIX_TOKENS) * B_WIDTH * 4 + MIX_TOKENS * D_MODEL * 2
    temps = MIX_TOKENS * (D_PROJ * 4 + D_MODEL * 4 * 2) + 4 * 1024 * 1024
    outs = pl.pallas_call(
        functools.partial(_mixer_kernel, n_cast=len(cast_jobs)),
        grid=(bsz, seq_steps),
        in_specs=[
            pl.BlockSpec((None, MIX_TOKENS, D_MODEL), tile),
            pl.BlockSpec((None, 1, D_MODEL), pick3),
            pl.BlockSpec((D_MODEL, D_PROJ), fixed, pipeline_mode=once),
            pl.BlockSpec((D_MODEL, D_MODEL), fixed, pipeline_mode=once),
            pl.BlockSpec((None, 1, A_WIDTH), pick3),
            pl.BlockSpec((None, A_GROUPS, CHUNK, CHUNK), lambda b, j: (layer, 0, 0, 0)),
            pl.BlockSpec((None, CHUNK, A_WIDTH), pick3),
            pl.BlockSpec((None, CONV_W, B_WIDTH), pick3),
            pl.BlockSpec((None, 1, C_WIDTH), pick3),
            pl.BlockSpec((2, MIX_TOKENS, C_HEAD), lambda b, j: (0, j, 0)),
            pl.BlockSpec(ret.shape, lambda b, j: (0, 0, 0, 0)),
        ] + c_in,
        out_specs=[pl.BlockSpec((None, MIX_TOKENS, D_MODEL), tile)] + c_out,
        out_shape=[jax.ShapeDtypeStruct(x3.shape, jnp.float32)] + c_shapes,
        scratch_shapes=[
            pltpu.VMEM((C_HEADS, C_HEAD, C_HEAD), jnp.float32),
            pltpu.VMEM((CONV_HALO + MIX_TOKENS, B_WIDTH), jnp.float32),
            pltpu.VMEM((MIX_TOKENS, D_MODEL), jnp.bfloat16),
        ],
        compiler_params=pltpu.CompilerParams(
            dimension_semantics=("arbitrary", "arbitrary"),
            vmem_limit_bytes=_vmem_limit(blocks, weight_bytes + scratch, temps)),
        name="mixer_block",
    )(x3, norm, *weights, vnorm, ws, bias, convw, gn, rope, ret, *[a for a, _ in cast_jobs])
    return outs[0], outs[1:]


def _rope_tables(seq):
    half = C_HEAD // 2
    inv = ROPE_BASE ** (-np.arange(half, dtype=np.float64) / half)
    ang = np.arange(seq, dtype=np.float64)[:, None] * inv[None, :]
    cos, sin = np.cos(ang), np.sin(ang)
    cc = np.concatenate([cos, cos], axis=-1)
    ss = np.concatenate([-sin, sin], axis=-1)
    return jnp.asarray(np.stack([cc, ss]), jnp.float32)


def _retention_tables():
    log_g = np.log1p(-np.exp2(-5.0 - np.arange(C_HEADS, dtype=np.float64)))
    idx = np.arange(CHUNK, dtype=np.float64)
    diff = idx[:, None] - idx[None, :]
    scale = C_HEAD ** -0.5
    decay = scale * np.where(diff[None] >= 0, np.exp(log_g[:, None, None] * np.maximum(diff, 0.0)[None]), 0.0)
    zeta = scale * np.exp(log_g[:, None] * (CHUNK - 1 - idx)[None, :])
    xi = np.exp(log_g[:, None] * (idx + 1.0)[None, :])
    gamma = np.exp(log_g * CHUNK)
    full = (C_HEADS, CHUNK, CHUNK)
    return jnp.asarray(np.stack([decay,
                                 np.broadcast_to(zeta[:, :, None], full),
                                 np.broadcast_to(xi[:, :, None], full),
                                 np.broadcast_to(gamma[:, None, None], full)]), jnp.float32)


def kernel(x, ffn1_norm, ffn1_w_gate, ffn1_w_up, ffn1_w_down, mix_norm, w_in, gmlp_v_norm, gmlp_w_s,
           gmlp_b_s, conv_w, ret_gn, w_out, ffn2_norm, ffn2_w_gate, ffn2_w_up, ffn2_w_down, final_norm):
    bsz, seq, d = x.shape
    depth = w_in.shape[0]
    assert d == D_MODEL and seq % MIX_TOKENS == 0 and (bsz * seq) % FFN_TOKENS == 0
    rope = _rope_tables(seq)
    ret = _retention_tables()
    fnorm = final_norm.reshape(1, d)
    ffn1_n, ffn2_n, mix_n = (n.reshape(depth, 1, d) for n in (ffn1_norm, ffn2_norm, mix_norm))
    bias = jnp.repeat(jnp.swapaxes(gmlp_b_s, 1, 2), A_HEAD, axis=2)
    small = (gmlp_v_norm.reshape(depth, 1, A_WIDTH), gmlp_w_s, bias, conv_w, ret_gn.reshape(depth, 1, C_WIDTH))
    ffn1_f32 = (ffn1_w_gate, ffn1_w_up, ffn1_w_down)
    ffn2_f32 = (ffn2_w_gate, ffn2_w_up, ffn2_w_down)
    mix_f32 = (w_in, w_out)
    w_next = [w[0].astype(jnp.bfloat16) for w in ffn1_f32]
    x = x.reshape(bsz * seq, d)
    for l in range(depth):
        last = l == depth - 1
        x, w_next = _ffn(x, l, ffn1_n, w_next, fnorm, [(w, l) for w in mix_f32], apply_final_norm=False)
        x, w_next = _mixer(x.reshape(bsz, seq, d), l, mix_n, w_next, *small, rope, ret,
                           [(w, l) for w in ffn2_f32])
        x, w_next = _ffn(x.reshape(bsz * seq, d), l, ffn2_n, w_next, fnorm,
                         [] if last else [(w, l + 1) for w in ffn1_f32], apply_final_norm=last)
    return x.reshape(bsz, seq, d)
```

```python
import functools

import jax
import jax.numpy as jnp
import numpy as np
from jax import lax
from jax.experimental import pallas as pl
from jax.experimental.pallas import tpu as pltpu

D_MODEL = 1024
CHUNK = 128
A_GROUPS = 4
A_WIDTH = 256
A_HEAD = A_WIDTH // A_GROUPS
B_WIDTH = 256
CONV_W = 3
C_HEADS = 4
C_WIDTH = 512
C_HEAD = C_WIDTH // C_HEADS
ROPE_BASE = 10000.0
D_FF = 2816
MACARON = 0.5
EPS = 1e-6

A_END = 2 * A_WIDTH
B_END = A_END + 3 * B_WIDTH
D_PROJ = B_END + 4 * C_WIDTH

V7X_VMEM_BYTES = 64 * 1024 * 1024
V7X_MXU_DIM = 256
V7X_SUBLANES = 8
V7X_BF16_SUBLANES = 2 * V7X_SUBLANES

FFN_TOKENS = 1024
FFN_SUBTILE = 512
FFN_SKEW = 5
FFN_CHUNK = V7X_MXU_DIM
MIX_TOKENS = 1024
MIX_SUBTILE = 512
MIX_SKEW = 4
CONV_HALO = V7X_SUBLANES


def _vmem_limit(block_bytes, scratch_bytes, temp_bytes):
    want = 2 * block_bytes + scratch_bytes + temp_bytes
    return int(min(want, V7X_VMEM_BYTES - 8 * 1024 * 1024))


def _rms(x, gain):
    return x * lax.rsqrt(jnp.mean(x * x, axis=-1, keepdims=True) + EPS) * gain


def _cast_specs(jobs, n_steps, step_of):
    in_specs, out_specs, out_shapes, nbytes = [], [], [], 0
    for arr, layer in jobs:
        _, rows, cols = arr.shape
        n_blocks = n_steps
        while rows % n_blocks or (rows // n_blocks) % V7X_BF16_SUBLANES:
            n_blocks //= 2
        br = rows // n_blocks

        def src_map(*idx, layer=layer, last=n_blocks - 1):
            return (layer, jnp.minimum(step_of(*idx), last), 0)

        def dst_map(*idx, last=n_blocks - 1):
            return (jnp.minimum(step_of(*idx), last), 0)

        in_specs.append(pl.BlockSpec((None, br, cols), src_map))
        out_specs.append(pl.BlockSpec((br, cols), dst_map))
        out_shapes.append(jax.ShapeDtypeStruct((rows, cols), jnp.bfloat16))
        nbytes += br * cols * (4 + 2)
    return in_specs, out_specs, out_shapes, nbytes


def _cast_blocks(src_refs, dst_refs):
    for src, dst in zip(src_refs, dst_refs):
        dst[...] = src[...].astype(jnp.bfloat16)


def _ffn_kernel(*refs, n_cast, apply_final_norm):
    x_ref, norm_ref, wg_ref, wu_ref, wd_ref, fnorm_ref = refs[:6]
    o_ref = refs[6 + n_cast]
    n_sub = FFN_TOKENS // FFN_SUBTILE
    n_chunks = D_FF // FFN_CHUNK
    rows = lambda s: slice(s * FFN_SUBTILE, (s + 1) * FFN_SUBTILE)

    h = [None] * n_sub
    acc = [None] * n_sub
    for t in range(n_chunks + FFN_SKEW * (n_sub - 1)):
        for s in range(n_sub):
            c = t - s * FFN_SKEW
            if not 0 <= c < n_chunks:
                continue
            if c == 0:
                h[s] = _rms(x_ref[rows(s), :], norm_ref[...]).astype(jnp.bfloat16)
                acc[s] = jnp.zeros((FFN_SUBTILE, D_MODEL), jnp.float32)
            cols = slice(c * FFN_CHUNK, (c + 1) * FFN_CHUNK)
            gate = jnp.dot(h[s], wg_ref[:, cols], preferred_element_type=jnp.float32)
            up = jnp.dot(h[s], wu_ref[:, cols], preferred_element_type=jnp.float32)
            if s == 0 and c == 0:
                _cast_blocks(refs[6:6 + n_cast], refs[7 + n_cast:])
            a = (jax.nn.silu(gate) * up).astype(jnp.bfloat16)
            acc[s] = acc[s] + jnp.dot(a, wd_ref[cols, :], preferred_element_type=jnp.float32)
            if c == n_chunks - 1:
                y = x_ref[rows(s), :] + MACARON * acc[s]
                if apply_final_norm:
                    y = _rms(y, fnorm_ref[...])
                o_ref[rows(s), :] = y


def _ffn(x2, layer, norm, weights, fnorm, cast_jobs, *, apply_final_norm):
    n_tok = x2.shape[0]
    n_steps = n_tok // FFN_TOKENS
    row = lambda i: (i, 0)
    fixed = lambda i: (0, 0)
    once = pl.Buffered(1)
    c_in, c_out, c_shapes, c_bytes = _cast_specs(cast_jobs, n_steps, lambda i: i)
    tiles = FFN_TOKENS * D_MODEL * 4 * 2 + c_bytes
    weight_bytes = 3 * D_MODEL * D_FF * 2
    temps = FFN_TOKENS * (D_MODEL * 4 * 3 + 2 * FFN_CHUNK * 4 * 4)
    outs = pl.pallas_call(
        functools.partial(_ffn_kernel, n_cast=len(cast_jobs), apply_final_norm=apply_final_norm),
        grid=(n_steps,),
        in_specs=[
            pl.BlockSpec((FFN_TOKENS, D_MODEL), row),
            pl.BlockSpec((None, 1, D_MODEL), lambda i: (layer, 0, 0)),
            pl.BlockSpec((D_MODEL, D_FF), fixed, pipeline_mode=once),
            pl.BlockSpec((D_MODEL, D_FF), fixed, pipeline_mode=once),
            pl.BlockSpec((D_FF, D_MODEL), fixed, pipeline_mode=once),
            pl.BlockSpec((1, D_MODEL), fixed),
        ] + c_in,
        out_specs=[pl.BlockSpec((FFN_TOKENS, D_MODEL), row)] + c_out,
        out_shape=[jax.ShapeDtypeStruct(x2.shape, jnp.float32)] + c_shapes,
        compiler_params=pltpu.CompilerParams(
            dimension_semantics=("arbitrary",),
            vmem_limit_bytes=_vmem_limit(tiles, weight_bytes, temps)),
        name="ffn_half_step",
    )(x2, norm, *weights, fnorm, *[a for a, _ in cast_jobs])
    return outs[0], outs[1:]


def _split_bf16(v):
    hi = v.astype(jnp.bfloat16)
    lo = (v - hi.astype(jnp.float32)).astype(jnp.bfloat16)
    return hi, lo


def _group_mean(v, avg):
    hi, lo = _split_bf16(v)
    return (jnp.dot(hi, avg, preferred_element_type=jnp.float32)
            + jnp.dot(lo, avg, preferred_element_type=jnp.float32))


def _lane_norm(x, gain):
    mu = jnp.mean(x, axis=-1, keepdims=True)
    d = x - mu
    var = jnp.mean(d * d, axis=-1, keepdims=True)
    return d * lax.rsqrt(var + EPS) * gain


N_MIXER_INPUTS = 11


def _mixer_kernel(*refs, n_cast):
    (x_ref, norm_ref, win_ref, wout_ref, vnorm_ref, ws_ref, bias_ref, convw_ref, gn_ref, rope_ref,
     ret_ref) = refs[:N_MIXER_INPUTS]
    o_ref = refs[N_MIXER_INPUTS + n_cast]
    state_ref, zbuf_ref, y_ref = refs[N_MIXER_INPUTS + 2 * n_cast + 1:]
    f32, bf16 = jnp.float32, jnp.bfloat16
    first = pl.program_id(1) == 0

    @pl.when(first)
    def _():
        state_ref[...] = jnp.zeros(state_ref.shape, f32)
        zbuf_ref[0:CONV_HALO, :] = jnp.zeros((CONV_HALO, B_WIDTH), f32)

    T = MIX_SUBTILE
    n_sub = MIX_TOKENS // T
    n_chunks = T // CHUNK
    col_c = A_WIDTH + B_WIDTH

    head_shift = A_HEAD.bit_length() - 1
    lane_grp = lax.shift_right_logical(lax.broadcasted_iota(jnp.int32, (A_WIDTH, A_WIDTH), 1), head_shift)
    row_grp = lax.shift_right_logical(lax.broadcasted_iota(jnp.int32, (A_WIDTH, A_WIDTH), 0), head_shift)
    avg = jnp.where(lane_grp == row_grp, 1.0 / A_HEAD, 0.0).astype(bf16)
    t_idx = lax.broadcasted_iota(jnp.int32, (CHUNK, CHUNK), 0)
    s_idx = lax.broadcasted_iota(jnp.int32, (CHUNK, CHUNK), 1)
    ws = jnp.concatenate([jnp.where(s_idx <= t_idx, ws_ref[g], 0.0) for g in range(A_GROUPS)],
                         axis=1).astype(bf16)
    out_grp = lax.shift_right_logical(lax.broadcasted_iota(jnp.int32, (CHUNK, A_WIDTH), 1), head_shift)

    def sub_tile(sub):
        r0 = sub * T
        tile_rows = slice(r0, r0 + T)
        x = x_ref[tile_rows, :]
        h = _rms(x, norm_ref[...]).astype(bf16)
        pa = jnp.dot(h, win_ref[:, 0:A_END], preferred_element_type=f32)
        pb = jnp.dot(h, win_ref[:, A_END:B_END], preferred_element_type=f32)
        if sub == 0:
            _cast_blocks(refs[N_MIXER_INPUTS:N_MIXER_INPUTS + n_cast],
                         refs[N_MIXER_INPUTS + n_cast + 1:N_MIXER_INPUTS + 2 * n_cast + 1])
        yield

        z = jax.nn.gelu(pa, approximate=True)
        u, v = z[:, :A_WIDTH], z[:, A_WIDTH:]
        mu = _group_mean(v, avg)
        pqk = jnp.dot(h, win_ref[:, B_END:B_END + 2 * C_WIDTH], preferred_element_type=f32)
        yield
        d = v - mu
        var = _group_mean(d * d, avg)
        pv = jnp.dot(h, win_ref[:, B_END + 2 * C_WIDTH:B_END + 3 * C_WIDTH], preferred_element_type=f32)
        yield
        vn = (d * lax.rsqrt(var + EPS) * vnorm_ref[...]).astype(bf16)
        for n in range(n_chunks):
            rows = slice(n * CHUNK, (n + 1) * CHUNK)
            vc = vn[rows]
            rhs = jnp.concatenate([jnp.where(out_grp == g, vc, jnp.zeros_like(vc)) for g in range(A_GROUPS)],
                                  axis=0)
            sv = jnp.dot(ws, rhs, preferred_element_type=f32) + bias_ref[...]
            y_ref[r0 + n * CHUNK:r0 + (n + 1) * CHUNK, 0:A_WIDTH] = (u[rows] * sv).astype(bf16)

        bg, cg, xin = pb[:, :B_WIDTH], pb[:, B_WIDTH:2 * B_WIDTH], pb[:, 2 * B_WIDTH:]
        zc = cg * xin
        z0 = CONV_HALO + r0
        zbuf_ref[z0:z0 + T, :] = zc
        conv = zc * convw_ref[CONV_W - 1:CONV_W, :]
        for i in range(CONV_W - 1):
            lag = CONV_W - 1 - i
            conv = conv + zbuf_ref[z0 - lag:z0 - lag + T, :] * convw_ref[i:i + 1, :]
        y_ref[tile_rows, A_WIDTH:col_c] = (bg * conv).astype(bf16)
        if sub == n_sub - 1:
            zbuf_ref[0:CONV_HALO, :] = zbuf_ref[MIX_TOKENS:MIX_TOKENS + CONV_HALO, :]

        cos, sin = rope_ref[0, tile_rows, :], rope_ref[1, tile_rows, :]
        q_rot, v_bf, scores, kvs = [], [], [], []
        for hd in range(C_HEADS):
            lanes = slice(hd * C_HEAD, (hd + 1) * C_HEAD)
            zeta = ret_ref[1, hd]
            qh = pqk[:, lanes]
            kh = pqk[:, C_WIDTH + hd * C_HEAD:C_WIDTH + (hd + 1) * C_HEAD]
            qh = qh * cos + pltpu.roll(qh, C_HEAD // 2, 1) * sin
            kh = kh * cos + pltpu.roll(kh, C_HEAD // 2, 1) * sin
            vb = pv[:, lanes].astype(bf16)
            q_rot.append(qh)
            v_bf.append(vb)
            for n in range(n_chunks):
                rows = slice(n * CHUNK, (n + 1) * CHUNK)
                kc = kh[rows]
                scores.append(lax.dot_general(qh[rows].astype(bf16), kc.astype(bf16),
                                              (((1,), (1,)), ((), ())), preferred_element_type=f32))
                kvs.append(lax.dot_general((kc * zeta).astype(bf16), vb[rows], (((0,), (0,)), ((), ())),
                                           preferred_element_type=f32))
        yield

        pg = jnp.dot(h, win_ref[:, B_END + 3 * C_WIDTH:D_PROJ], preferred_element_type=f32)
        yield

        rets = []
        for hd in range(C_HEADS):
            decay, xi, gamma = ret_ref[0, hd], ret_ref[2, hd], ret_ref[3, hd]
            state = state_ref[hd]
            for n in range(n_chunks):
                rows = slice(n * CHUNK, (n + 1) * CHUNK)
                i = hd * n_chunks + n
                lhs = jnp.concatenate([(scores[i] * decay).astype(bf16), (q_rot[hd][rows] * xi).astype(bf16)],
                                      axis=1)
                rhs = jnp.concatenate([v_bf[hd][rows], state.astype(bf16)], axis=0)
                rets.append(jnp.dot(lhs, rhs, preferred_element_type=f32))
                state = state * gamma + kvs[i]
            state_ref[hd] = state
        yield

        out = x + jnp.dot(y_ref[tile_rows, 0:col_c], wout_ref[0:col_c, :], preferred_element_type=f32)
        yield
        for hd in range(C_HEADS):
            lanes = slice(hd * C_HEAD, (hd + 1) * C_HEAD)
            gate = jax.nn.silu(pg[:, lanes])
            gn = gn_ref[:, lanes]
            for n in range(n_chunks):
                rows = slice(n * CHUNK, (n + 1) * CHUNK)
                ret = _lane_norm(rets[hd * n_chunks + n], gn)
                y_ref[r0 + n * CHUNK:r0 + (n + 1) * CHUNK, col_c + hd * C_HEAD:col_c + (hd + 1) * C_HEAD] = (
                    gate[rows] * ret).astype(bf16)
        yield
        o_ref[tile_rows, :] = out + jnp.dot(y_ref[tile_rows, col_c:], wout_ref[col_c:, :],
                                            preferred_element_type=f32)

    tiles = [sub_tile(s) for s in range(n_sub)]
    live = [True] * n_sub
    t = 0
    while any(live):
        for s in range(n_sub):
            if live[s] and t >= s * MIX_SKEW:
                live[s] = next(tiles[s], "done") != "done"
        t += 1


def _mixer(x3, layer, norm, weights, vnorm, ws, bias, convw, gn, rope, ret, cast_jobs):
    bsz, seq, _ = x3.shape
    seq_steps = seq // MIX_TOKENS
    tile = lambda b, j: (b, j, 0)
    pick3 = lambda b, j: (layer, 0, 0)
    fixed = lambda b, j: (0, 0)
    once = pl.Buffered(1)
    c_in, c_out, c_shapes, c_bytes = _cast_specs(cast_jobs, bsz * seq_steps, lambda b, j: b * seq_steps + j)
    blocks = (MIX_TOKENS * D_MODEL * 4 * 2 + A_GROUPS * CHUNK * CHUNK * 4 + CHUNK * A_WIDTH * 4
              + 2 * MIX_TOKENS * C_HEAD * 4 + ret.size * 4 + c_bytes)
    weight_bytes = (D_MODEL * D_PROJ + D_MODEL * D_MODEL) * 2
    scratch = C_HEADS * C_HEAD * C_HEAD * 4 + (CONV_HALO + MIX_TOKENS) * B_WIDTH * 4 + MIX_TOKENS * D_MODEL * 2
    temps = MIX_TOKENS * (D_PROJ * 4 + D_MODEL * 4 * 2) + 4 * 1024 * 1024
    outs = pl.pallas_call(
        functools.partial(_mixer_kernel, n_cast=len(cast_jobs)),
        grid=(bsz, seq_steps),
        in_specs=[
            pl.BlockSpec((None, MIX_TOKENS, D_MODEL), tile),
            pl.BlockSpec((None, 1, D_MODEL), pick3),
            pl.BlockSpec((D_MODEL, D_PROJ), fixed, pipeline_mode=once),
            pl.BlockSpec((D_MODEL, D_MODEL), fixed, pipeline_mode=once),
            pl.BlockSpec((None, 1, A_WIDTH), pick3),
            pl.BlockSpec((None, A_GROUPS, CHUNK, CHUNK), lambda b, j: (layer, 0, 0, 0)),
            pl.BlockSpec((None, CHUNK, A_WIDTH), pick3),
            pl.BlockSpec((None, CONV_W, B_WIDTH), pick3),
            pl.BlockSpec((None, 1, C_WIDTH), pick3),
            pl.BlockSpec((2, MIX_TOKENS, C_HEAD), lambda b, j: (0, j, 0)),
            pl.BlockSpec(ret.shape, lambda b, j: (0, 0, 0, 0)),
        ] + c_in,
        out_specs=[pl.BlockSpec((None, MIX_TOKENS, D_MODEL), tile)] + c_out,
        out_shape=[jax.ShapeDtypeStruct(x3.shape, jnp.float32)] + c_shapes,
        scratch_shapes=[
            pltpu.VMEM((C_HEADS, C_HEAD, C_HEAD), jnp.float32),
            pltpu.VMEM((CONV_HALO + MIX_TOKENS, B_WIDTH), jnp.float32),
            pltpu.VMEM((MIX_TOKENS, D_MODEL), jnp.bfloat16),
        ],
        compiler_params=pltpu.CompilerParams(
            dimension_semantics=("arbitrary", "arbitrary"),
            vmem_limit_bytes=_vmem_limit(blocks, weight_bytes + scratch, temps)),
        name="mixer_block",
    )(x3, norm, *weights, vnorm, ws, bias, convw, gn, rope, ret, *[a for a, _ in cast_jobs])
    return outs[0], outs[1:]


def _rope_tables(seq):
    half = C_HEAD // 2
    inv = ROPE_BASE ** (-np.arange(half, dtype=np.float64) / half)
    ang = np.arange(seq, dtype=np.float64)[:, None] * inv[None, :]
    cos, sin = np.cos(ang), np.sin(ang)
    cc = np.concatenate([cos, cos], axis=-1)
    ss = np.concatenate([-sin, sin], axis=-1)
    return jnp.asarray(np.stack([cc, ss]), jnp.float32)


def _retention_tables():
    log_g = np.log1p(-np.exp2(-5.0 - np.arange(C_HEADS, dtype=np.float64)))
    idx = np.arange(CHUNK, dtype=np.float64)
    diff = idx[:, None] - idx[None, :]
    scale = C_HEAD ** -0.5
    decay = scale * np.where(diff[None] >= 0, np.exp(log_g[:, None, None] * np.maximum(diff, 0.0)[None]), 0.0)
    zeta = scale * np.exp(log_g[:, None] * (CHUNK - 1 - idx)[None, :])
    xi = np.exp(log_g[:, None] * (idx + 1.0)[None, :])
    gamma = np.exp(log_g * CHUNK)
    full = (C_HEADS, CHUNK, CHUNK)
    return jnp.asarray(np.stack([decay,
                                 np.broadcast_to(zeta[:, :, None], full),
                                 np.broadcast_to(xi[:, :, None], full),
                                 np.broadcast_to(gamma[:, None, None], full)]), jnp.float32)


def kernel(x, ffn1_norm, ffn1_w_gate, ffn1_w_up, ffn1_w_down, mix_norm, w_in, gmlp_v_norm, gmlp_w_s,
           gmlp_b_s, conv_w, ret_gn, w_out, ffn2_norm, ffn2_w_gate, ffn2_w_up, ffn2_w_down, final_norm):
    bsz, seq, d = x.shape
    depth = w_in.shape[0]
    assert d == D_MODEL and seq % MIX_TOKENS == 0 and (bsz * seq) % FFN_TOKENS == 0
    rope = _rope_tables(seq)
    ret = _retention_tables()
    fnorm = final_norm.reshape(1, d)
    ffn1_n, ffn2_n, mix_n = (n.reshape(depth, 1, d) for n in (ffn1_norm, ffn2_norm, mix_norm))
    bias = jnp.repeat(jnp.swapaxes(gmlp_b_s, 1, 2), A_HEAD, axis=2)
    small = (gmlp_v_norm.reshape(depth, 1, A_WIDTH), gmlp_w_s, bias, conv_w, ret_gn.reshape(depth, 1, C_WIDTH))
    ffn1_f32 = (ffn1_w_gate, ffn1_w_up, ffn1_w_down)
    ffn2_f32 = (ffn2_w_gate, ffn2_w_up, ffn2_w_down)
    mix_f32 = (w_in, w_out)
    w_next = [w[0].astype(jnp.bfloat16) for w in ffn1_f32]
    x = x.reshape(bsz * seq, d)
    for l in range(depth):
        last = l == depth - 1
        x, w_next = _ffn(x, l, ffn1_n, w_next, fnorm, [(w, l) for w in mix_f32], apply_final_norm=False)
        x, w_next = _mixer(x.reshape(bsz, seq, d), l, mix_n, w_next, *small, rope, ret,
                           [(w, l) for w in ffn2_f32])
        x, w_next = _ffn(x.reshape(bsz * seq, d), l, ffn2_n, w_next, fnorm,
                         [] if last else [(w, l + 1) for w in ffn1_f32], apply_final_norm=last)
    return x.reshape(bsz, seq, d)
```
